```python
import math
import jax, jax.numpy as jnp
from jax import lax
import numpy as np

D_MODEL = 1024
BATCH = 8
SEQ = 2048
DEPTH = 4

CTX_LEN = 256
GRID_W = 64
D_SSM = 512
SSM_CH = 16
SSM_GROUPS = D_SSM // SSM_CH
SSM_STATE = 64
D_SGU = D_MODEL - D_SSM
SGU_HEADS = 4
SGU_HEAD_DIM = D_SGU // SGU_HEADS
CHUNK = 128
D_FF = 2816
CONV_K = 3
EPS = 1e-6
D_IN = D_SSM + 2 * D_SGU

kernel_name = "hybrid_s5_gmlp_convffn_prefix_dit"


def rmsnorm(x, g):
    xf = x.astype(jnp.float32)
    y = xf * lax.rsqrt(jnp.mean(jnp.square(xf), axis=-1, keepdims=True) + EPS)
    return (y * g.astype(jnp.float32)).astype(x.dtype)


def modulate(h, shift, scale):
    return h * (1.0 + scale) + shift


def _combine(left, right):
    a_l, b_l = left
    a_r, b_r = right
    return a_r * a_l, a_r * b_l + b_r


def diag_scan(a_bar, bu, reverse):
    a = jnp.broadcast_to(a_bar, bu.shape)
    _, h = lax.associative_scan(_combine, (a, bu), reverse=reverse, axis=1)
    return h


def s5_bidirectional(u_ctx, u_lat, a_re, a_im, b_re, b_im, c_re, c_im, log_dt, d_skip, need_ctx):
    f32 = jnp.float32
    n_b, l_ctx, _ = u_ctx.shape
    l_lat = u_lat.shape[1]
    uc = u_ctx.astype(f32)
    ul = u_lat.astype(f32)
    uc_g = uc.reshape(n_b, l_ctx, SSM_GROUPS, SSM_CH).astype(jnp.complex64)
    ul_g = ul.reshape(n_b, l_lat, SSM_GROUPS, SSM_CH).astype(jnp.complex64)
    d = d_skip.astype(f32)
    y_lat = ul * d
    y_ctx = uc * d if need_ctx else None
    for direction in range(2):
        reverse = direction == 1
        a = lax.complex(a_re[direction].astype(f32), a_im[direction].astype(f32))
        a_dt = a * jnp.exp(log_dt[direction].astype(f32))[:, None]
        a_bar = jnp.exp(a_dt)
        b_mat = lax.complex(b_re[direction].astype(f32), b_im[direction].astype(f32))
        b_bar = ((a_bar - 1.0) / a)[:, :, None] * b_mat
        c_mat = lax.complex(c_re[direction].astype(f32), c_im[direction].astype(f32))
        h_ctx = diag_scan(a_bar, jnp.einsum("blgc,gpc->blgp", uc_g, b_bar), reverse)
        h0 = h_ctx[:, 0] if reverse else h_ctx[:, -1]
        steps = (jnp.arange(l_lat, 0, -1) if reverse else jnp.arange(1, l_lat + 1)).astype(f32)
        carry = jnp.exp(steps[:, None, None] * a_dt)
        h_lat = diag_scan(a_bar, jnp.einsum("blgc,gpc->blgp", ul_g, b_bar), reverse) + carry[None] * h0[:, None]
        y_lat = y_lat + jnp.real(jnp.einsum("blgp,gcp->blgc", h_lat, c_mat)).reshape(n_b, l_lat, D_SSM)
        if need_ctx:
            y_ctx = y_ctx + jnp.real(jnp.einsum("blgp,gcp->blgc", h_ctx, c_mat)).reshape(n_b, l_ctx, D_SSM)
    return y_ctx, y_lat


def s5_glu(y, w_glu, b_glu, dtype):
    y = jax.nn.gelu(y)
    return (y * jax.nn.sigmoid(y @ w_glu.astype(jnp.float32) + b_glu.astype(jnp.float32))).astype(dtype)


def spatial_gating(u, v, g_sgu, w_spatial, b_spatial):
    n_b, length, _ = u.shape
    v = rmsnorm(v, g_sgu).reshape(n_b, length // CHUNK, CHUNK, SGU_HEADS, SGU_HEAD_DIM)
    mixed = jnp.einsum("hpq,bnqhd->bnphd", w_spatial, v) + b_spatial.T[None, None, :, :, None]
    return u * mixed.reshape(n_b, length, D_SGU)


def conv_ffn(h, w_up, w_conv, w_down, rows):
    n_b, length, _ = h.shape
    up = (h @ w_up).reshape(n_b, rows, length // rows, 2 * D_FF)
    up = lax.conv_general_dilated(up, w_conv[:, :, None, :], (1, 1), "SAME",
                                  dimension_numbers=("NHWC", "HWIO", "NHWC"),
                                  feature_group_count=2 * D_FF)
    gate, val = jnp.split(up.reshape(n_b, length, 2 * D_FF), 2, axis=-1)
    return (jax.nn.silu(gate) * val) @ w_down


def mixer_inputs(h, w_in):
    p = h @ w_in
    u_ssm = p[..., :D_SSM]
    u_sgu = jax.nn.gelu(p[..., D_SSM:D_SSM + D_SGU])
    v_sgu = jax.nn.gelu(p[..., D_SSM + D_SGU:])
    return u_ssm, u_sgu, v_sgu


def setup_inputs(seed: int = 0) -> dict:
    key = jax.random.key(seed)
    ks = jax.random.split(key, 27)
    f32 = jnp.float32

    def nrm(k, shape, scale):
        return scale * jax.random.normal(k, shape, f32)

    G, P, CH = SSM_GROUPS, SSM_STATE, SSM_CH
    a_im = math.pi * jnp.broadcast_to(jnp.arange(P, dtype=f32), (DEPTH, 2, G, P)) + nrm(ks[9], (DEPTH, 2, G, P), 0.01)
    return {
        "x": nrm(ks[0], (BATCH, SEQ, D_MODEL), 1.0),
        "c": nrm(ks[1], (BATCH, D_MODEL), 1.0),
        "ctx": nrm(ks[2], (BATCH, CTX_LEN, D_MODEL), 1.0),
        "c_ctx": nrm(ks[3], (D_MODEL,), 1.0),
        "w_ada": nrm(ks[4], (DEPTH, D_MODEL, 6 * D_MODEL), 0.5 * D_MODEL ** -0.5),
        "b_ada": nrm(ks[5], (DEPTH, 6 * D_MODEL), 0.01),
        "g_mix": 1.0 + nrm(ks[6], (DEPTH, D_MODEL), 0.01),
        "w_in": nrm(ks[7], (DEPTH, D_MODEL, D_IN), D_MODEL ** -0.5),
        "ssm_a_re": -0.5 + nrm(ks[8], (DEPTH, 2, G, P), 0.01),
        "ssm_a_im": a_im,
        "ssm_b_re": nrm(ks[10], (DEPTH, 2, G, P, CH), (2 * CH) ** -0.5),
        "ssm_b_im": nrm(ks[11], (DEPTH, 2, G, P, CH), (2 * CH) ** -0.5),
        "ssm_c_re": nrm(ks[12], (DEPTH, 2, G, CH, P), P ** -0.5),
        "ssm_c_im": nrm(ks[13], (DEPTH, 2, G, CH, P), P ** -0.5),
        "ssm_log_dt": jax.random.uniform(ks[14], (DEPTH, 2, G), f32, math.log(1e-3), math.log(1e-1)),
        "ssm_d": nrm(ks[15], (DEPTH, D_SSM), 1.0),
        "w_glu": nrm(ks[16], (DEPTH, D_SSM, D_SSM), D_SSM ** -0.5),
        "b_glu": nrm(ks[17], (DEPTH, D_SSM), 0.01),
        "g_sgu": 1.0 + nrm(ks[18], (DEPTH, D_SGU), 0.01),
        "w_spatial": nrm(ks[19], (DEPTH, SGU_HEADS, CHUNK, CHUNK), 0.5 * CHUNK ** -0.5),
        "b_spatial": 1.0 + nrm(ks[20], (DEPTH, SGU_HEADS, CHUNK), 0.01),
        "w_out": nrm(ks[21], (DEPTH, D_MODEL, D_MODEL), D_MODEL ** -0.5),
        "g_ffn": 1.0 + nrm(ks[22], (DEPTH, D_MODEL), 0.01),
        "w_up": nrm(ks[23], (DEPTH, D_MODEL, 2 * D_FF), D_MODEL ** -0.5),
        "w_conv": nrm(ks[24], (DEPTH, CONV_K, CONV_K, 2 * D_FF), 1.0 / CONV_K),
        "w_down": nrm(ks[25], (DEPTH, D_FF, D_MODEL), D_FF ** -0.5),
        "g_final": 1.0 + nrm(ks[26], (D_MODEL,), 0.01),
    }


def reference(x, c, ctx, c_ctx, w_ada, b_ada, g_mix, w_in, ssm_a_re, ssm_a_im, ssm_b_re, ssm_b_im,
              ssm_c_re, ssm_c_im, ssm_log_dt, ssm_d, w_glu, b_glu, g_sgu, w_spatial, b_spatial,
              w_out, g_ffn, w_up, w_conv, w_down, g_final):
    rows = x.shape[1] // GRID_W
    cond_lat = jax.nn.silu(c)[:, None, :]
    cond_ctx = jax.nn.silu(c_ctx)[None, None, :]
    xc = ctx
    for i in range(DEPTH):
        need_ctx = i < DEPTH - 1
        mod_l = jnp.split(cond_lat @ w_ada[i] + b_ada[i], 6, axis=-1)
        mod_c = jnp.split(cond_ctx @ w_ada[i] + b_ada[i], 6, axis=-1)
        h_l = modulate(rmsnorm(x, g_mix[i]), mod_l[0], mod_l[1])
        h_c = modulate(rmsnorm(xc, g_mix[i]), mod_c[0], mod_c[1])
        us_l, ug_l, vg_l = mixer_inputs(h_l, w_in[i])
        if need_ctx:
            us_c, ug_c, vg_c = mixer_inputs(h_c, w_in[i])
        else:
            us_c = h_c @ w_in[i][:, :D_SSM]
        y_ssm_c, y_ssm_l = s5_bidirectional(us_c, us_l, ssm_a_re[i], ssm_a_im[i], ssm_b_re[i], ssm_b_im[i],
                                            ssm_c_re[i], ssm_c_im[i], ssm_log_dt[i], ssm_d[i], need_ctx)
        mix_l = jnp.concatenate([s5_glu(y_ssm_l, w_glu[i], b_glu[i], x.dtype),
                                 spatial_gating(ug_l, vg_l, g_sgu[i], w_spatial[i], b_spatial[i])], axis=-1)
        x = x + mod_l[2] * (mix_l @ w_out[i])
        hf_l = modulate(rmsnorm(x, g_ffn[i]), mod_l[3], mod_l[4])
        x = x + mod_l[5] * conv_ffn(hf_l, w_up[i], w_conv[i], w_down[i], rows)
        if need_ctx:
            mix_c = jnp.concatenate([s5_glu(y_ssm_c, w_glu[i], b_glu[i], xc.dtype),
                                     spatial_gating(ug_c, vg_c, g_sgu[i], w_spatial[i], b_spatial[i])], axis=-1)
            xc = xc + mod_c[2] * (mix_c @ w_out[i])
            hf_c = modulate(rmsnorm(xc, g_ffn[i]), mod_c[3], mod_c[4])
            xc = xc + mod_c[5] * conv_ffn(hf_c, w_up[i], w_conv[i], w_down[i], 1)
    return rmsnorm(x, g_final)
```

```python
import functools

import jax
import jax.numpy as jnp
from jax import lax
from jax.experimental import pallas as pl
from jax.experimental.pallas import tpu as pltpu

D_MODEL = 1024
BATCH = 8
SEQ = 2048
DEPTH = 4
CTX_LEN = 256
D_SSM = 512
SSM_CH = 16
SSM_GROUPS = 32
SSM_STATE = 64
D_SGU = 512
SGU_HEADS = 4
SGU_HEAD_DIM = 128
CHUNK = 128
D_FF = 2816
EPS = 1e-6
D_IN = D_SSM + 2 * D_SGU

T_S5 = 16
J_S5 = CHUNK // T_S5
BLK = 2048
M_BLK = BLK // CHUNK
N_COL = BLK // T_S5
N_STATE = SSM_GROUPS * SSM_STATE
SUB = 512
N_SUB = BLK // SUB
FF_TILE = 256
N_FF = D_FF // FF_TILE
LANE = 128
VMEM_LIMIT = 56 * 1024 * 1024

F32 = jnp.float32
BF16 = jnp.bfloat16


def _rms(x):
    return x * lax.rsqrt(jnp.mean(x * x, axis=-1, keepdims=True) + EPS)


def _mod_kernel(c_ref, w_ref, b_ref, o_ref):
    cond = jax.nn.silu(c_ref[...]).astype(BF16)
    o_ref[0] = jnp.dot(cond, w_ref[0].astype(BF16), preferred_element_type=F32) + b_ref[0]


def _modulation(cond, w_ada, b_ada):
    rows = cond.shape[0]
    n_out = w_ada.shape[-1] // D_MODEL
    return pl.pallas_call(
        _mod_kernel,
        grid=(DEPTH, n_out),
        in_specs=[
            pl.BlockSpec((rows, D_MODEL), lambda i, j: (0, 0)),
            pl.BlockSpec((1, D_MODEL, D_MODEL), lambda i, j: (i, 0, j)),
            pl.BlockSpec((1, 1, D_MODEL), lambda i, j: (i, 0, j)),
        ],
        out_specs=pl.BlockSpec((1, rows, D_MODEL), lambda i, j: (i, 0, j)),
        out_shape=jax.ShapeDtypeStruct((DEPTH, rows, n_out * D_MODEL), F32),
        name="adaln_mod",
    )(cond, w_ada, b_ada.reshape(DEPTH, 1, -1))


def _mixer_in_kernel(x_ref, mod_ref, g_ref, win_ref, gsgu_ref, ws_ref, bs_ref,
                     r_ref, sgu_ref, u_scr, *, with_sgu):
    j = pl.program_id(1)
    xn = _rms(x_ref[0]) * g_ref[...]
    hm = (xn * (1.0 + mod_ref[0, 1:2, :]) + mod_ref[0, 0:1, :]).astype(BF16)
    if with_sgu:
        p = jnp.dot(hm, win_ref[...], preferred_element_type=F32)
    else:
        p = jnp.dot(hm, win_ref[:, :D_SSM], preferred_element_type=F32)
    u_scr[pl.ds(j * (SUB // CHUNK), SUB // CHUNK)] = (
        p[:, :D_SSM].reshape(SUB // CHUNK, T_S5, J_S5, D_SSM))
    if with_sgu:
        ug = jax.nn.gelu(p[:, D_SSM:D_SSM + D_SGU])
        vn = (_rms(jax.nn.gelu(p[:, D_SSM + D_SGU:])) * gsgu_ref[...]).astype(BF16)
        for c in range(SUB // CHUNK):
            rows = slice(c * CHUNK, (c + 1) * CHUNK)
            for h in range(SGU_HEADS):
                cols = slice(h * SGU_HEAD_DIM, (h + 1) * SGU_HEAD_DIM)
                mixed = jnp.dot(ws_ref[h], vn[rows, cols], preferred_element_type=F32) + bs_ref[h]
                sgu_ref[0, rows, cols] = (ug[rows, cols] * mixed).astype(BF16)

    @pl.when(j == N_SUB - 1)
    def _():
        for s in range(T_S5):
            us = u_scr[:, s, :, :].reshape(N_COL, D_SSM)
            r_ref[0, :, s * SSM_CH:(s + 1) * SSM_CH, :] = (
                us.T.astype(BF16).reshape(SSM_GROUPS, SSM_CH, N_COL))


def _mixer_in(x, mod, g_mix, w_in, g_sgu, ws, bs, *, with_sgu):
    nblk = x.shape[0]
    const = lambda *shape: pl.BlockSpec(shape, lambda b, j: (0,) * len(shape))
    out_shape = [jax.ShapeDtypeStruct((nblk, SSM_GROUPS, T_S5 * SSM_CH, N_COL), BF16)]
    out_specs = [pl.BlockSpec((1, SSM_GROUPS, T_S5 * SSM_CH, N_COL), lambda b, j: (b, 0, 0, 0))]
    if with_sgu:
        out_shape.append(jax.ShapeDtypeStruct((nblk, BLK, D_SGU), BF16))
        out_specs.append(pl.BlockSpec((1, SUB, D_SGU), lambda b, j: (b, j, 0)))
    kern = functools.partial(_mixer_in_kernel, with_sgu=with_sgu)
    if not with_sgu:
        kern = lambda x_ref, mod_ref, g_ref, win_ref, gsgu_ref, ws_ref, bs_ref, r_ref, u_scr: (
            _mixer_in_kernel(x_ref, mod_ref, g_ref, win_ref, gsgu_ref, ws_ref, bs_ref,
                             r_ref, None, u_scr, with_sgu=False))
    res = pl.pallas_call(
        kern,
        grid=(nblk, N_SUB),
        in_specs=[
            pl.BlockSpec((1, SUB, D_MODEL), lambda b, j: (b, j, 0)),
            pl.BlockSpec((1, 6, D_MODEL), lambda b, j: (b, 0, 0)),
            const(1, D_MODEL),
            const(D_MODEL, D_IN),
            const(1, D_SGU),
            const(SGU_HEADS, CHUNK, CHUNK),
            const(SGU_HEADS, CHUNK, 1),
        ],
        out_specs=out_specs,
        out_shape=out_shape,
        scratch_shapes=[pltpu.VMEM((M_BLK, T_S5, J_S5, D_SSM), F32)],
        compiler_params=pltpu.CompilerParams(
            dimension_semantics=("arbitrary", "arbitrary"), vmem_limit_bytes=VMEM_LIMIT),
        name="mixer_in",
    )(x, mod, g_mix, w_in, g_sgu, ws, bs)
    return res if with_sgu else (res[0], None)


SCAN_LANES = 512


def _s5_kernel(r_ref, mt_ref, pt_ref, qt_ref, a_ref, h0_ref, y_ref, hfin_ref,
               yt_scr, et_scr, e_scr, *, is_ctx, with_y):
    kinds = 4
    pstate = SSM_STATE

    def chunk_mm(g, carry):
        rg = r_ref[0, g]
        if with_y:
            yt_scr[g] = jnp.dot(mt_ref[g], rg, preferred_element_type=F32)
        et_scr[g] = jnp.dot(pt_ref[g], rg, preferred_element_type=F32)
        return carry

    lax.fori_loop(0, SSM_GROUPS, chunk_mm, 0)

    for k in range(kinds):
        blk = et_scr[:, k * pstate:(k + 1) * pstate, :].reshape(N_STATE, N_COL)
        e_scr[k] = blk.T

    per_seq = CTX_LEN // T_S5
    for lb in range(N_STATE // SCAN_LANES):
        lanes = slice(lb * SCAN_LANES, (lb + 1) * SCAN_LANES)
        af_re, af_im = a_ref[0:1, lanes], a_ref[1:2, lanes]
        ab_re, ab_im = a_ref[2:3, lanes], a_ref[3:4, lanes]
        if is_ctx:
            init = tuple(jnp.zeros((1, SCAN_LANES), F32) for _ in range(kinds))
        else:
            init = tuple(h0_ref[0, k:k + 1, lanes] for k in range(kinds))

        def step(i, carry, lanes=lanes, af_re=af_re, af_im=af_im, ab_re=ab_re, ab_im=ab_im):
            f_re, f_im, b_re, b_im = carry
            cf = i
            cb = N_COL - 1 - i
            if is_ctx:
                first = lax.rem(cf, per_seq) == 0
                f_re = jnp.where(first, 0.0, f_re)
                f_im = jnp.where(first, 0.0, f_im)
                b_re = jnp.where(first, 0.0, b_re)
                b_im = jnp.where(first, 0.0, b_im)
            ef_re = e_scr[0, pl.ds(cf, 1), lanes]
            ef_im = e_scr[1, pl.ds(cf, 1), lanes]
            eb_re = e_scr[2, pl.ds(cb, 1), lanes]
            eb_im = e_scr[3, pl.ds(cb, 1), lanes]
            e_scr[0, pl.ds(cf, 1), lanes] = f_re
            e_scr[1, pl.ds(cf, 1), lanes] = f_im
            e_scr[2, pl.ds(cb, 1), lanes] = b_re
            e_scr[3, pl.ds(cb, 1), lanes] = b_im
            nf_re = af_re * f_re - af_im * f_im + ef_re
            nf_im = af_re * f_im + af_im * f_re + ef_im
            nb_re = ab_re * b_re - ab_im * b_im + eb_re
            nb_im = ab_re * b_im + ab_im * b_re + eb_im
            if is_ctx:
                hfin_ref[0, pl.ds(cf // per_seq, 1), lanes] = nf_re
                hfin_ref[1, pl.ds(cf // per_seq, 1), lanes] = nf_im
                hfin_ref[2, pl.ds(cb // per_seq, 1), lanes] = nb_re
                hfin_ref[3, pl.ds(cb // per_seq, 1), lanes] = nb_im
            return nf_re, nf_im, nb_re, nb_im

        lax.fori_loop(0, N_COL, step, init)

    if with_y:
        for k in range(kinds):
            et_scr[:, k * pstate:(k + 1) * pstate, :] = (
                e_scr[k].T.reshape(SSM_GROUPS, pstate, N_COL))

        def carry_mm(g, carry):
            yt_scr[g] += jnp.dot(qt_ref[g], et_scr[g].astype(BF16), preferred_element_type=F32)
            return carry

        lax.fori_loop(0, SSM_GROUPS, carry_mm, 0)

        for t in range(T_S5):
            slab = yt_scr[:, t * SSM_CH:(t + 1) * SSM_CH, :].reshape(D_SSM, N_COL)
            y_ref[0, :, t, :, :] = slab.T.reshape(M_BLK, J_S5, D_SSM)


def _s5_core(r, mt, pt, qt, avec, h0, *, is_ctx, with_y):
    nblk = r.shape[0]
    wspec = pl.BlockSpec((SSM_GROUPS, 256, 256), lambda b: (0, 0, 0), pipeline_mode=pl.Buffered(1))
    out_shape, out_specs = [], []
    if with_y:
        out_shape.append(jax.ShapeDtypeStruct((nblk, M_BLK, T_S5, J_S5, D_SSM), F32))
        out_specs.append(pl.BlockSpec((1, M_BLK, T_S5, J_S5, D_SSM), lambda b: (b, 0, 0, 0, 0)))
    if is_ctx:
        out_shape.append(jax.ShapeDtypeStruct((4, BATCH, N_STATE), F32))
        out_specs.append(pl.BlockSpec((4, BATCH, N_STATE), lambda b: (0, 0, 0)))

    def kern(r_ref, mt_ref, pt_ref, qt_ref, a_ref, h0_ref, *rest):
        outs, scr = rest[:len(out_shape)], rest[len(out_shape):]
        y_ref = outs[0] if with_y else None
        hfin_ref = outs[-1] if is_ctx else None
        _s5_kernel(r_ref, mt_ref, pt_ref, qt_ref, a_ref, h0_ref, y_ref, hfin_ref, *scr,
                   is_ctx=is_ctx, with_y=with_y)

    res = pl.pallas_call(
        kern,
        grid=(nblk,),
        in_specs=[
            pl.BlockSpec((1, SSM_GROUPS, 256, N_COL), lambda b: (b, 0, 0, 0)),
            wspec, wspec, wspec,
            pl.BlockSpec((4, N_STATE), lambda b: (0, 0)),
            pl.BlockSpec((1, 4, N_STATE), lambda b: (b, 0, 0)),
        ],
        out_specs=out_specs,
        out_shape=out_shape,
        scratch_shapes=[
            pltpu.VMEM((SSM_GROUPS, 256, N_COL), F32),
            pltpu.VMEM((SSM_GROUPS, 256, N_COL), F32),
            pltpu.VMEM((4, N_COL, N_STATE), F32),
        ],
        compiler_params=pltpu.CompilerParams(
            dimension_semantics=("arbitrary",), vmem_limit_bytes=VMEM_LIMIT),
        name="s5_core_ctx" if is_ctx else "s5_core",
    )(r, mt, pt, qt, avec, h0)
    y = res[0].reshape(nblk, BLK, D_SSM) if with_y else None
    hfin = res[-1] if is_ctx else None
    return y, hfin


def _mixer_out_kernel(x_ref, y_ref, sgu_ref, mod_ref, wglu_ref, bglu_ref, wout_ref, o_ref):
    yg = jax.nn.gelu(y_ref[0])
    z = jnp.dot(yg.astype(BF16), wglu_ref[...], preferred_element_type=F32) + bglu_ref[...]
    glu = (yg * jax.nn.sigmoid(z)).astype(BF16)
    delta = jnp.dot(glu, wout_ref[:D_SSM, :], preferred_element_type=F32)
    delta += jnp.dot(sgu_ref[0], wout_ref[D_SSM:, :], preferred_element_type=F32)
    o_ref[0] = x_ref[0] + mod_ref[0, 2:3, :] * delta


def _mixer_out(x, y, sgu, mod, w_glu, b_glu, w_out):
    nblk = x.shape[0]
    const = lambda *shape: pl.BlockSpec(shape, lambda b, j: (0,) * len(shape))
    return pl.pallas_call(
        _mixer_out_kernel,
        grid=(nblk, N_SUB),
        in_specs=[
            pl.BlockSpec((1, SUB, D_MODEL), lambda b, j: (b, j, 0)),
            pl.BlockSpec((1, SUB, D_SSM), lambda b, j: (b, j, 0)),
            pl.BlockSpec((1, SUB, D_SGU), lambda b, j: (b, j, 0)),
            pl.BlockSpec((1, 6, D_MODEL), lambda b, j: (b, 0, 0)),
            const(D_SSM, D_SSM),
            const(1, D_SSM),
            const(D_MODEL, D_MODEL),
        ],
        out_specs=pl.BlockSpec((1, SUB, D_MODEL), lambda b, j: (b, j, 0)),
        out_shape=jax.ShapeDtypeStruct(x.shape, F32),
        input_output_aliases={0: 0},
        compiler_params=pltpu.CompilerParams(
            dimension_semantics=("arbitrary", "arbitrary"), vmem_limit_bytes=VMEM_LIMIT),
        name="mixer_out",
    )(x, y, sgu, mod, w_glu, b_glu, w_out)


def _conv_tiles(up_scr, w_ref, m, lanes, *, is_ctx):
    sub = lax.broadcasted_iota(jnp.int32, (J_S5, LANE), 0)
    w = [jnp.broadcast_to(w_ref[k:k + 1, lanes], (J_S5, LANE)) for k in range(9)]

    def taps(mm, s):
        x = up_scr[mm, s, :, lanes]
        if is_ctx:
            return [w[3 + dw] * x for dw in range(3)]
        lo = up_scr[jnp.maximum(mm - 1, 0), s, :, lanes]
        hi = up_scr[jnp.minimum(mm + 1, M_BLK - 1), s, :, lanes]
        rx = pltpu.roll(x, 4, 0)
        up = jnp.where(sub < 4, pltpu.roll(lo, 4, 0), rx)
        up = jnp.where(jnp.logical_or(sub >= 4, mm > 0), up, 0.0)
        dn = jnp.where(sub < 4, rx, pltpu.roll(hi, 4, 0))
        dn = jnp.where(jnp.logical_or(sub < 4, mm < M_BLK - 1), dn, 0.0)
        return [w[dw] * up + w[3 + dw] * x + w[6 + dw] * dn for dw in range(3)]

    v = [taps(m, s) for s in range(T_S5)]
    out = []
    for s in range(T_S5):
        acc = v[s][1]
        if s > 0:
            acc = acc + v[s - 1][0]
        else:
            edge = pltpu.roll(v[T_S5 - 1][0], 1, 0)
            if is_ctx:
                prev = pltpu.roll(taps(jnp.maximum(m - 1, 0), T_S5 - 1)[0], 1, 0)
                prev = jnp.where(lax.rem(m, 2) == 1, prev, 0.0)
                edge = jnp.where(sub == 0, prev, edge)
            else:
                edge = jnp.where(lax.rem(sub, 4) == 0, 0.0, edge)
            acc = acc + edge
        if s < T_S5 - 1:
            acc = acc + v[s + 1][2]
        else:
            edge = pltpu.roll(v[0][2], J_S5 - 1, 0)
            if is_ctx:
                nxt = pltpu.roll(taps(jnp.minimum(m + 1, M_BLK - 1), 0)[2], J_S5 - 1, 0)
                nxt = jnp.where(lax.rem(m, 2) == 0, nxt, 0.0)
                edge = jnp.where(sub == J_S5 - 1, nxt, edge)
            else:
                edge = jnp.where(lax.rem(sub, 4) == 3, 0.0, edge)
            acc = acc + edge
        out.append(acc)
    return out


def _ffn_kernel(x_ref, mod_ref, g_ref, wg_ref, wv_ref, cg_ref, cv_ref, wd_ref, o_ref,
                hf_scr, upg_scr, upv_scr, act_scr, *, is_ctx):
    jf = pl.program_id(1)

    @pl.when(jf == 0)
    def _():
        def norm(i, carry):
            rows = pl.ds(pl.multiple_of(i * SUB, SUB), SUB)
            xn = _rms(x_ref[0, rows, :]) * g_ref[...]
            hf_scr[rows, :] = (xn * (1.0 + mod_ref[0, 4:5, :]) + mod_ref[0, 3:4, :]).astype(BF16)
            return carry
        lax.fori_loop(0, N_SUB, norm, 0)

    hf = hf_scr[...]
    upg_scr[...] = jnp.dot(hf, wg_ref[...], preferred_element_type=F32).reshape(
        M_BLK, T_S5, J_S5, FF_TILE)
    upv_scr[...] = jnp.dot(hf, wv_ref[...], preferred_element_type=F32).reshape(
        M_BLK, T_S5, J_S5, FF_TILE)

    def conv(i, carry):
        m = i // (FF_TILE // LANE)
        cb = lax.rem(i, FF_TILE // LANE)
        lanes = pl.ds(pl.multiple_of(cb * LANE, LANE), LANE)
        gate = _conv_tiles(upg_scr, cg_ref, m, lanes, is_ctx=is_ctx)
        val = _conv_tiles(upv_scr, cv_ref, m, lanes, is_ctx=is_ctx)
        for s in range(T_S5):
            act_scr[m, s, :, lanes] = jax.nn.silu(gate[s]) * val[s]
        return carry

    lax.fori_loop(0, M_BLK * (FF_TILE // LANE), conv, 0)

    down = jnp.dot(act_scr[...].reshape(BLK, FF_TILE).astype(BF16), wd_ref[...],
                   preferred_element_type=F32)

    @pl.when(jf == 0)
    def _():
        o_ref[0] = down

    @pl.when(jf > 0)
    def _():
        o_ref[0] += down

    @pl.when(jf == N_FF - 1)
    def _():
        o_ref[0] = x_ref[0] + mod_ref[0, 5:6, :] * o_ref[0]


def _ffn(x, mod, g_ffn, w_up, w_conv, w_down, *, is_ctx):
    nblk = x.shape[0]
    return pl.pallas_call(
        functools.partial(_ffn_kernel, is_ctx=is_ctx),
        grid=(nblk, N_FF),
        in_specs=[
            pl.BlockSpec((1, BLK, D_MODEL), lambda b, j: (b, 0, 0)),
            pl.BlockSpec((1, 6, D_MODEL), lambda b, j: (b, 0, 0)),
            pl.BlockSpec((1, D_MODEL), lambda b, j: (0, 0)),
            pl.BlockSpec((D_MODEL, FF_TILE), lambda b, j: (0, j)),
            pl.BlockSpec((D_MODEL, FF_TILE), lambda b, j: (0, N_FF + j)),
            pl.BlockSpec((9, FF_TILE), lambda b, j: (0, j)),
            pl.BlockSpec((9, FF_TILE), lambda b, j: (0, N_FF + j)),
            pl.BlockSpec((FF_TILE, D_MODEL), lambda b, j: (j, 0)),
        ],
        out_specs=pl.BlockSpec((1, BLK, D_MODEL), lambda b, j: (b, 0, 0)),
        out_shape=jax.ShapeDtypeStruct(x.shape, F32),
        scratch_shapes=[
            pltpu.VMEM((BLK, D_MODEL), BF16),
            pltpu.VMEM((M_BLK, T_S5, J_S5, FF_TILE), F32),
            pltpu.VMEM((M_BLK, T_S5, J_S5, FF_TILE), F32),
            pltpu.VMEM((M_BLK, T_S5, J_S5, FF_TILE), F32),
        ],
        compiler_params=pltpu.CompilerParams(
            dimension_semantics=("arbitrary", "arbitrary"), vmem_limit_bytes=VMEM_LIMIT),
        name="ffn_ctx" if is_ctx else "ffn",
    )(x, mod, g_ffn, w_up, w_up, w_conv, w_conv, w_down)


def _final_kernel(x_ref, g_ref, o_ref):
    o_ref[0] = _rms(x_ref[0]) * g_ref[...]


def _final_norm(x, g):
    nblk = x.shape[0]
    return pl.pallas_call(
        _final_kernel,
        grid=(nblk, N_SUB),
        in_specs=[pl.BlockSpec((1, SUB, D_MODEL), lambda b, j: (b, j, 0)),
                  pl.BlockSpec((1, D_MODEL), lambda b, j: (0, 0))],
        out_specs=pl.BlockSpec((1, SUB, D_MODEL), lambda b, j: (b, j, 0)),
        out_shape=jax.ShapeDtypeStruct(x.shape, F32),
        name="final_norm",
    )(x, g)


def _s5_operators(a_re, a_im, b_re, b_im, c_re, c_im, log_dt, d_skip):
    hp = lax.Precision.HIGHEST
    dt = jnp.exp(log_dt)[..., None]
    adt_re, adt_im = a_re * dt, a_im * dt
    k = jnp.arange(T_S5 + 1, dtype=F32).reshape(-1, 1, 1, 1, 1)
    mag = jnp.exp(k * adt_re)
    pw_re, pw_im = mag * jnp.cos(k * adt_im), mag * jnp.sin(k * adt_im)
    num_re, num_im = pw_re[1] - 1.0, pw_im[1]
    den = a_re * a_re + a_im * a_im
    cf_re = (num_re * a_re + num_im * a_im) / den
    cf_im = (num_im * a_re - num_re * a_im) / den
    bb_re = cf_re[..., None] * b_re - cf_im[..., None] * b_im
    bb_im = cf_re[..., None] * b_im + cf_im[..., None] * b_re
    z_re = pw_re[..., None] * bb_re - pw_im[..., None] * bb_im
    z_im = pw_re[..., None] * bb_im + pw_im[..., None] * bb_re

    kern = (jnp.einsum("edgcp,kedgpi->kedgci", c_re, z_re[:T_S5], precision=hp)
            - jnp.einsum("edgcp,kedgpi->kedgci", c_im, z_im[:T_S5], precision=hp))
    kf, kb = kern[:, :, 0], kern[:, :, 1]
    skip = d_skip.reshape(DEPTH, SSM_GROUPS, SSM_CH)[..., None] * jnp.eye(SSM_CH, dtype=F32)
    comb = jnp.concatenate([kb[:0:-1], (kf[0] + kb[0] + skip)[None], kf[1:]], axis=0)
    lag = jnp.arange(T_S5)[:, None] - jnp.arange(T_S5)[None, :] + (T_S5 - 1)
    mt = comb[lag]
    mt = mt.transpose(2, 3, 0, 4, 1, 5).reshape(DEPTH, SSM_GROUPS, 256, 256)

    zf_re, zf_im = z_re[T_S5 - 1::-1, :, 0], z_im[T_S5 - 1::-1, :, 0]
    zb_re, zb_im = z_re[:T_S5, :, 1], z_im[:T_S5, :, 1]
    pt = jnp.stack([zf_re, zf_im, zb_re, zb_im], axis=0)
    pt = pt.transpose(2, 3, 0, 4, 1, 5).reshape(DEPTH, SSM_GROUPS, 256, 256)

    def cq(direction, pr, pi):
        cr, ci = c_re[:, direction], c_im[:, direction]
        pr, pi = pr[:, :, direction, :, None, :], pi[:, :, direction, :, None, :]
        return cr * pr - ci * pi, cr * pi + ci * pr

    qf_re, qf_im = cq(0, pw_re[1:], pw_im[1:])
    qb_re, qb_im = cq(1, pw_re[:0:-1], pw_im[:0:-1])
    qt = jnp.stack([qf_re, -qf_im, qb_re, -qb_im], axis=4)
    qt = qt.transpose(1, 2, 0, 3, 4, 5).reshape(DEPTH, SSM_GROUPS, 256, 256)

    a16 = jnp.stack([pw_re[T_S5, :, 0], pw_im[T_S5, :, 0], pw_re[T_S5, :, 1], pw_im[T_S5, :, 1]],
                    axis=1)
    return mt.astype(BF16), pt.astype(BF16), qt.astype(BF16), a16.reshape(DEPTH, 4, N_STATE)


def _chunk_perm():
    i = jnp.arange(CHUNK)
    return 16 * (i % J_S5) + i // J_S5


def _to_blocks(x, chunks):
    b = x.shape[0]
    x = x.reshape(b, chunks, J_S5, T_S5, D_MODEL).transpose(0, 1, 3, 2, 4)
    return x.reshape(b, chunks * CHUNK, D_MODEL)


def _from_blocks(x, chunks):
    b = x.shape[0]
    x = x.reshape(b, chunks, T_S5, J_S5, D_MODEL).transpose(0, 1, 3, 2, 4)
    return x.reshape(b, chunks * CHUNK, D_MODEL)


def kernel(x, c, ctx, c_ctx, w_ada, b_ada, g_mix, w_in, ssm_a_re, ssm_a_im, ssm_b_re, ssm_b_im,
           ssm_c_re, ssm_c_im, ssm_log_dt, ssm_d, w_glu, b_glu, g_sgu, w_spatial, b_spatial,
           w_out, g_ffn, w_up, w_conv, w_down, g_final):
    xl = _to_blocks(x, SEQ // CHUNK)
    xc = _to_blocks(ctx, CTX_LEN // CHUNK).reshape(1, BLK, D_MODEL)

    cond = jnp.concatenate([c_ctx[None], c, jnp.zeros((7, D_MODEL), F32)], axis=0)
    mods = _modulation(cond, w_ada, b_ada).reshape(DEPTH, 16, 6, D_MODEL)

    mt, pt, qt, avec = _s5_operators(ssm_a_re, ssm_a_im, ssm_b_re, ssm_b_im, ssm_c_re, ssm_c_im,
                                     ssm_log_dt, ssm_d)
    perm = _chunk_perm()
    ws = w_spatial[:, :, perm][:, :, :, perm].astype(BF16)
    bs = b_spatial[:, :, perm][..., None]
    w_in_b, w_glu_b, w_out_b = w_in.astype(BF16), w_glu.astype(BF16), w_out.astype(BF16)
    w_up_b, w_down_b = w_up.astype(BF16), w_down.astype(BF16)
    w_conv9 = w_conv.reshape(DEPTH, 9, 2 * D_FF)
    zero_h0 = jnp.zeros((1, 4, N_STATE), F32)

    for i in range(DEPTH):
        need_ctx = i < DEPTH - 1
        mod_c, mod_l = mods[i, 0:1], mods[i, 1:1 + BATCH]
        gm, gs, gf = g_mix[i][None], g_sgu[i][None], g_ffn[i][None]
        bg = b_glu[i][None]

        r_c, sgu_c = _mixer_in(xc, mod_c, gm, w_in_b[i], gs, ws[i], bs[i], with_sgu=need_ctx)
        y_c, hfin = _s5_core(r_c, mt[i], pt[i], qt[i], avec[i], zero_h0, is_ctx=True, with_y=need_ctx)
        h0 = hfin.transpose(1, 0, 2)

        r_l, sgu_l = _mixer_in(xl, mod_l, gm, w_in_b[i], gs, ws[i], bs[i], with_sgu=True)
        y_l, _ = _s5_core(r_l, mt[i], pt[i], qt[i], avec[i], h0, is_ctx=False, with_y=True)
        xl = _mixer_out(xl, y_l, sgu_l, mod_l, w_glu_b[i], bg, w_out_b[i])
        xl = _ffn(xl, mod_l, gf, w_up_b[i], w_conv9[i], w_down_b[i], is_ctx=False)

        if need_ctx:
            xc = _mixer_out(xc, y_c, sgu_c, mod_c, w_glu_b[i], bg, w_out_b[i])
            xc = _ffn(xc, mod_c, gf, w_up_b[i], w_conv9[i], w_down_b[i], is_ctx=True)

    out = _final_norm(xl, g_final[None])
    return _from_blocks(out, SEQ // CHUNK)
```

```python
import functools

import jax
import jax.numpy as jnp
from jax import lax
from jax.experimental import pallas as pl
from jax.experimental.pallas import tpu as pltpu

D_MODEL = 1024
BATCH = 8
SEQ = 2048
DEPTH = 4
CTX_LEN = 256
D_SSM = 512
SSM_CH = 16
SSM_GROUPS = 32
SSM_STATE = 64
D_SGU = 512
SGU_HEADS = 4
SGU_HEAD_DIM = 128
CHUNK = 128
D_FF = 2816
EPS = 1e-6
D_IN = D_SSM + 2 * D_SGU

T_S5 = 16
J_S5 = CHUNK // T_S5
BLK = 2048
M_BLK = BLK // CHUNK
N_COL = BLK // T_S5
N_STATE = SSM_GROUPS * SSM_STATE
S5_MAT = T_S5 * SSM_CH
SUB = 512
N_SUB = BLK // SUB
FF_TILE = 256
N_FF = D_FF // FF_TILE
RG = 1024
N_RG = BLK // RG
LANE = 128
VMEM_LIMIT = 56 * 1024 * 1024

F32 = jnp.float32
BF16 = jnp.bfloat16


def _rms(x):
    return x * lax.rsqrt(jnp.mean(x * x, axis=-1, keepdims=True) + EPS)


def _layer_spec(shape, layer, **kw):
    zeros = (0,) * len(shape)
    return pl.BlockSpec((None,) + tuple(shape), lambda *_: (layer,) + zeros, **kw)


def _mod_spec(layer, is_ctx):
    if is_ctx:
        return pl.BlockSpec((None, 1, 6, D_MODEL), lambda b, *_: (layer, 0, 0, 0))
    return pl.BlockSpec((None, 1, 6, D_MODEL), lambda b, *_: (layer, 1 + b, 0, 0))


def _mod_kernel(c_ref, w_ref, b_ref, o_ref):
    cond = jax.nn.silu(c_ref[...]).astype(BF16)
    o_ref[0] = jnp.dot(cond, w_ref[0].astype(BF16), preferred_element_type=F32) + b_ref[0]


def _modulation(cond, w_ada, b_ada):
    rows = cond.shape[0]
    n_out = w_ada.shape[-1] // D_MODEL
    return pl.pallas_call(
        _mod_kernel,
        grid=(DEPTH, n_out),
        in_specs=[
            pl.BlockSpec((rows, D_MODEL), lambda i, j: (0, 0)),
            pl.BlockSpec((1, D_MODEL, D_MODEL), lambda i, j: (i, 0, j)),
            pl.BlockSpec((1, 1, D_MODEL), lambda i, j: (i, 0, j)),
        ],
        out_specs=pl.BlockSpec((1, rows, D_MODEL), lambda i, j: (i, 0, j)),
        out_shape=jax.ShapeDtypeStruct((DEPTH, rows, n_out * D_MODEL), F32),
        name="adaln_mod",
    )(cond, w_ada, b_ada.reshape(DEPTH, 1, -1))


def _mixer_in_kernel(x_ref, mod_ref, g_ref, win_ref, gsgu_ref, ws_ref, bs_ref,
                     r_ref, sgu_ref, u_scr, *, with_sgu):
    j = pl.program_id(1)
    xn = _rms(x_ref[0]) * g_ref[...]
    hm = (xn * (1.0 + mod_ref[0, 1:2, :]) + mod_ref[0, 0:1, :]).astype(BF16)
    if with_sgu:
        p = jnp.dot(hm, win_ref[...], preferred_element_type=F32)
    else:
        p = jnp.dot(hm, win_ref[:, :D_SSM], preferred_element_type=F32)
    u_scr[pl.ds(j * (SUB // CHUNK), SUB // CHUNK)] = (
        p[:, :D_SSM].reshape(SUB // CHUNK, T_S5, J_S5, D_SSM))
    if with_sgu:
        ug = jax.nn.gelu(p[:, D_SSM:D_SSM + D_SGU])
        vn = (_rms(jax.nn.gelu(p[:, D_SSM + D_SGU:])) * gsgu_ref[...]).astype(BF16)
        for c in range(SUB // CHUNK):
            rows = slice(c * CHUNK, (c + 1) * CHUNK)
            for h in range(SGU_HEADS):
                cols = slice(h * SGU_HEAD_DIM, (h + 1) * SGU_HEAD_DIM)
                mixed = jnp.dot(ws_ref[h], vn[rows, cols], preferred_element_type=F32) + bs_ref[h]
                sgu_ref[0, rows, cols] = (ug[rows, cols] * mixed).astype(BF16)

    @pl.when(j == N_SUB - 1)
    def _():
        for s in range(T_S5):
            us = u_scr[:, s, :, :].reshape(N_COL, D_SSM)
            r_ref[0, :, s * SSM_CH:(s + 1) * SSM_CH, :] = (
                us.T.astype(BF16).reshape(SSM_GROUPS, SSM_CH, N_COL))


def _mixer_in_kernel_ssm_only(x_ref, mod_ref, g_ref, win_ref, gsgu_ref, ws_ref, bs_ref,
                              r_ref, u_scr):
    _mixer_in_kernel(x_ref, mod_ref, g_ref, win_ref, gsgu_ref, ws_ref, bs_ref,
                     r_ref, None, u_scr, with_sgu=False)


def _mixer_in(x, mods, g_mix, w_in, g_sgu, ws, bs, *, layer, is_ctx, with_sgu):
    nblk = x.shape[0]
    out_shape = [jax.ShapeDtypeStruct((nblk, SSM_GROUPS, S5_MAT, N_COL), BF16)]
    out_specs = [pl.BlockSpec((1, SSM_GROUPS, S5_MAT, N_COL), lambda b, j: (b, 0, 0, 0))]
    if with_sgu:
        out_shape.append(jax.ShapeDtypeStruct((nblk, BLK, D_SGU), BF16))
        out_specs.append(pl.BlockSpec((1, SUB, D_SGU), lambda b, j: (b, j, 0)))
        kern = functools.partial(_mixer_in_kernel, with_sgu=True)
    else:
        kern = _mixer_in_kernel_ssm_only
    res = pl.pallas_call(
        kern,
        grid=(nblk, N_SUB),
        in_specs=[
            pl.BlockSpec((1, SUB, D_MODEL), lambda b, j: (b, j, 0)),
            _mod_spec(layer, is_ctx),
            _layer_spec((1, D_MODEL), layer),
            _layer_spec((D_MODEL, D_IN), layer),
            _layer_spec((1, D_SGU), layer),
            _layer_spec((SGU_HEADS, CHUNK, CHUNK), layer),
            _layer_spec((SGU_HEADS, CHUNK, 1), layer),
        ],
        out_specs=out_specs,
        out_shape=out_shape,
        scratch_shapes=[pltpu.VMEM((M_BLK, T_S5, J_S5, D_SSM), F32)],
        compiler_params=pltpu.CompilerParams(
            dimension_semantics=("arbitrary", "arbitrary"), vmem_limit_bytes=VMEM_LIMIT),
        name="mixer_in",
    )(x, mods, g_mix, w_in, g_sgu, ws, bs)
    return res if with_sgu else (res[0], None)


SCAN_LANES = 512


def _s5_kernel(r_ref, mt_ref, pt_ref, qt_ref, a_ref, h0_ref, y_ref, hfin_ref,
               yt_scr, et_scr, e_scr, *, is_ctx, with_y):
    kinds = 4
    pstate = SSM_STATE

    def chunk_mm(g, carry):
        rg = r_ref[0, g]
        if with_y:
            yt_scr[g] = jnp.dot(mt_ref[g], rg, preferred_element_type=F32)
        et_scr[g] = jnp.dot(pt_ref[g], rg, preferred_element_type=F32)
        return carry

    lax.fori_loop(0, SSM_GROUPS, chunk_mm, 0, unroll=4)

    for k in range(kinds):
        blk = et_scr[:, k * pstate:(k + 1) * pstate, :].reshape(N_STATE, N_COL)
        e_scr[k] = blk.T

    per_seq = CTX_LEN // T_S5
    for lb in range(N_STATE // SCAN_LANES):
        lanes = slice(lb * SCAN_LANES, (lb + 1) * SCAN_LANES)
        af_re, af_im = a_ref[0:1, lanes], a_ref[1:2, lanes]
        ab_re, ab_im = a_ref[2:3, lanes], a_ref[3:4, lanes]
        if is_ctx:
            init = tuple(jnp.zeros((1, SCAN_LANES), F32) for _ in range(kinds))
        else:
            init = tuple(h0_ref[0, k:k + 1, lanes] for k in range(kinds))

        def step(i, carry, lanes=lanes, af_re=af_re, af_im=af_im, ab_re=ab_re, ab_im=ab_im):
            f_re, f_im, b_re, b_im = carry
            cf = i
            cb = N_COL - 1 - i
            if is_ctx:
                first = lax.rem(cf, per_seq) == 0
                f_re = jnp.where(first, 0.0, f_re)
                f_im = jnp.where(first, 0.0, f_im)
                b_re = jnp.where(first, 0.0, b_re)
                b_im = jnp.where(first, 0.0, b_im)
            ef_re = e_scr[0, pl.ds(cf, 1), lanes]
            ef_im = e_scr[1, pl.ds(cf, 1), lanes]
            eb_re = e_scr[2, pl.ds(cb, 1), lanes]
            eb_im = e_scr[3, pl.ds(cb, 1), lanes]
            e_scr[0, pl.ds(cf, 1), lanes] = f_re
            e_scr[1, pl.ds(cf, 1), lanes] = f_im
            e_scr[2, pl.ds(cb, 1), lanes] = b_re
            e_scr[3, pl.ds(cb, 1), lanes] = b_im
            nf_re = af_re * f_re - af_im * f_im + ef_re
            nf_im = af_re * f_im + af_im * f_re + ef_im
            nb_re = ab_re * b_re - ab_im * b_im + eb_re
            nb_im = ab_re * b_im + ab_im * b_re + eb_im
            if is_ctx:
                hfin_ref[0, pl.ds(cf // per_seq, 1), lanes] = nf_re
                hfin_ref[1, pl.ds(cf // per_seq, 1), lanes] = nf_im
                hfin_ref[2, pl.ds(cb // per_seq, 1), lanes] = nb_re
                hfin_ref[3, pl.ds(cb // per_seq, 1), lanes] = nb_im
            return nf_re, nf_im, nb_re, nb_im

        lax.fori_loop(0, N_COL, step, init)

    if with_y:
        for k in range(kinds):
            et_scr[:, k * pstate:(k + 1) * pstate, :] = (
                e_scr[k].T.reshape(SSM_GROUPS, pstate, N_COL))

        def carry_mm(g, carry):
            yt_scr[g] += jnp.dot(qt_ref[g], et_scr[g].astype(BF16), preferred_element_type=F32)
            return carry

        lax.fori_loop(0, SSM_GROUPS, carry_mm, 0, unroll=4)

        for t in range(T_S5):
            slab = yt_scr[:, t * SSM_CH:(t + 1) * SSM_CH, :].reshape(D_SSM, N_COL)
            y_ref[0, :, t, :, :] = slab.T.reshape(M_BLK, J_S5, D_SSM)


def _s5_core(r, mt, pt, qt, avec, h0, *, layer, is_ctx, with_y):
    nblk = r.shape[0]
    wspec = _layer_spec((SSM_GROUPS, S5_MAT, S5_MAT), layer, pipeline_mode=pl.Buffered(1))
    out_shape, out_specs = [], []
    if with_y:
        out_shape.append(jax.ShapeDtypeStruct((nblk, M_BLK, T_S5, J_S5, D_SSM), F32))
        out_specs.append(pl.BlockSpec((1, M_BLK, T_S5, J_S5, D_SSM), lambda b: (b, 0, 0, 0, 0)))
    if is_ctx:
        out_shape.append(jax.ShapeDtypeStruct((4, BATCH, N_STATE), F32))
        out_specs.append(pl.BlockSpec((4, BATCH, N_STATE), lambda b: (0, 0, 0)))

    def kern(r_ref, mt_ref, pt_ref, qt_ref, a_ref, h0_ref, *rest):
        outs, scr = rest[:len(out_shape)], rest[len(out_shape):]
        y_ref = outs[0] if with_y else None
        hfin_ref = outs[-1] if is_ctx else None
        _s5_kernel(r_ref, mt_ref, pt_ref, qt_ref, a_ref, h0_ref, y_ref, hfin_ref, *scr,
                   is_ctx=is_ctx, with_y=with_y)

    res = pl.pallas_call(
        kern,
        grid=(nblk,),
        in_specs=[
            pl.BlockSpec((1, SSM_GROUPS, S5_MAT, N_COL), lambda b: (b, 0, 0, 0)),
            wspec, wspec, wspec,
            _layer_spec((4, N_STATE), layer),
            pl.BlockSpec((1, 4, N_STATE), lambda b: (b, 0, 0)),
        ],
        out_specs=out_specs,
        out_shape=out_shape,
        scratch_shapes=[
            pltpu.VMEM((SSM_GROUPS, S5_MAT, N_COL), F32),
            pltpu.VMEM((SSM_GROUPS, S5_MAT, N_COL), F32),
            pltpu.VMEM((4, N_COL, N_STATE), F32),
        ],
        compiler_params=pltpu.CompilerParams(
            dimension_semantics=("arbitrary",), vmem_limit_bytes=VMEM_LIMIT),
        name="s5_core_ctx" if is_ctx else "s5_core",
    )(r, mt, pt, qt, avec, h0)
    y = res[0].reshape(nblk, BLK, D_SSM) if with_y else None
    hfin = res[-1] if is_ctx else None
    return y, hfin


def _mixer_out_kernel(x_ref, y_ref, sgu_ref, mod_ref, wglu_ref, bglu_ref, wout_ref, o_ref):
    yg = jax.nn.gelu(y_ref[0])
    z = jnp.dot(yg.astype(BF16), wglu_ref[...], preferred_element_type=F32) + bglu_ref[...]
    glu = (yg * jax.nn.sigmoid(z)).astype(BF16)
    delta = jnp.dot(glu, wout_ref[:D_SSM, :], preferred_element_type=F32)
    delta += jnp.dot(sgu_ref[0], wout_ref[D_SSM:, :], preferred_element_type=F32)
    o_ref[0] = x_ref[0] + mod_ref[0, 2:3, :] * delta


def _mixer_out(x, y, sgu, mods, w_glu, b_glu, w_out, *, layer, is_ctx):
    nblk = x.shape[0]
    return pl.pallas_call(
        _mixer_out_kernel,
        grid=(nblk, N_SUB),
        in_specs=[
            pl.BlockSpec((1, SUB, D_MODEL), lambda b, j: (b, j, 0)),
            pl.BlockSpec((1, SUB, D_SSM), lambda b, j: (b, j, 0)),
            pl.BlockSpec((1, SUB, D_SGU), lambda b, j: (b, j, 0)),
            _mod_spec(layer, is_ctx),
            _layer_spec((D_SSM, D_SSM), layer),
            _layer_spec((1, D_SSM), layer),
            _layer_spec((D_MODEL, D_MODEL), layer),
        ],
        out_specs=pl.BlockSpec((1, SUB, D_MODEL), lambda b, j: (b, j, 0)),
        out_shape=jax.ShapeDtypeStruct(x.shape, F32),
        input_output_aliases={0: 0},
        compiler_params=pltpu.CompilerParams(
            dimension_semantics=("arbitrary", "arbitrary"), vmem_limit_bytes=VMEM_LIMIT),
        name="mixer_out",
    )(x, y, sgu, mods, w_glu, b_glu, w_out)


def _conv_tiles(up_scr, sh_scr, w_ref, m, lanes, *, is_ctx):
    sub = lax.broadcasted_iota(jnp.int32, (J_S5, LANE), 0)
    w = [jnp.broadcast_to(w_ref[k:k + 1, lanes], (J_S5, LANE)) for k in range(9)]

    def taps(mm, s):
        x = up_scr[mm, s, :, lanes]
        if is_ctx:
            return [w[3 + dw] * x for dw in range(3)]
        above = sh_scr[mm, s, :, lanes]
        below = sh_scr[mm + 1, s, :, lanes]
        return [w[dw] * above + w[3 + dw] * x + w[6 + dw] * below for dw in range(3)]

    v = [taps(m, s) for s in range(T_S5)]
    out = []
    for s in range(T_S5):
        acc = v[s][1]
        if s > 0:
            acc = acc + v[s - 1][0]
        else:
            edge = pltpu.roll(v[T_S5 - 1][0], 1, 0)
            if is_ctx:
                prev = pltpu.roll(taps(jnp.maximum(m - 1, 0), T_S5 - 1)[0], 1, 0)
                prev = jnp.where(lax.rem(m, 2) == 1, prev, 0.0)
                edge = jnp.where(sub == 0, prev, edge)
            else:
                edge = jnp.where(lax.rem(sub, 4) == 0, 0.0, edge)
            acc = acc + edge
        if s < T_S5 - 1:
            acc = acc + v[s + 1][2]
        else:
            edge = pltpu.roll(v[0][2], J_S5 - 1, 0)
            if is_ctx:
                nxt = pltpu.roll(taps(jnp.minimum(m + 1, M_BLK - 1), 0)[2], J_S5 - 1, 0)
                nxt = jnp.where(lax.rem(m, 2) == 0, nxt, 0.0)
                edge = jnp.where(sub == J_S5 - 1, nxt, edge)
            else:
                edge = jnp.where(lax.rem(sub, 4) == 3, 0.0, edge)
            acc = acc + edge
        out.append(acc)
    return out


def _ffn_kernel(x_ref, mod_ref, g_ref, wg_ref, wv_ref, cg_ref, cv_ref, wd_ref, o_ref,
                hf_scr, act_scr, *up_scrs, is_ctx):
    jf = pl.program_id(1)
    bufs = (up_scrs[0:4], up_scrs[4:8])
    half = J_S5 // 2

    @pl.when(jf == 0)
    def _():
        def norm(i, carry):
            rows = pl.ds(pl.multiple_of(i * SUB, SUB), SUB)
            xn = _rms(x_ref[0, rows, :]) * g_ref[...]
            hf_scr[rows, :] = (xn * (1.0 + mod_ref[0, 4:5, :]) + mod_ref[0, 3:4, :]).astype(BF16)
            o_ref[0, rows, :] = jnp.zeros((SUB, D_MODEL), F32)
            return carry
        lax.fori_loop(0, N_SUB, norm, 0)
        if not is_ctx:
            for bset in bufs:
                for sh_scr in bset[2:]:
                    sh_scr[0, :, 0:half, :] = jnp.zeros((T_S5, half, FF_TILE), F32)
                    sh_scr[M_BLK, :, half:J_S5, :] = jnp.zeros((T_S5, half, FF_TILE), F32)

    def stage_conv(rg, upg_scr, upv_scr, shg_scr, shv_scr):
        for c in range(RG // CHUNK):
            m = rg * (RG // CHUNK) + c
            for cb in range(FF_TILE // LANE):
                lanes = slice(cb * LANE, (cb + 1) * LANE)
                gate = _conv_tiles(upg_scr, shg_scr, cg_ref, m, lanes, is_ctx=is_ctx)
                val = _conv_tiles(upv_scr, shv_scr, cv_ref, m, lanes, is_ctx=is_ctx)
                act = [jax.nn.silu(g) * v for g, v in zip(gate, val)]
                for k in range(T_S5 // 2):
                    pair = jnp.concatenate([act[2 * k], act[2 * k + 1]], axis=0).astype(BF16)
                    row0 = pl.multiple_of(m * CHUNK + 2 * k * J_S5, 2 * J_S5)
                    act_scr[pl.ds(row0, 2 * J_S5), lanes] = pair

    def stage_up(rg, upg_scr, upv_scr, shg_scr, shv_scr):
        rows = pl.ds(pl.multiple_of(rg * RG, RG), RG)
        chunks = pl.ds(rg * (RG // CHUNK), RG // CHUNK)
        hf = hf_scr[rows, :]
        for w_ref, up_scr, sh_scr in ((wg_ref, upg_scr, shg_scr), (wv_ref, upv_scr, shv_scr)):
            up = jnp.dot(hf, w_ref[...], preferred_element_type=F32).reshape(
                RG // CHUNK, T_S5, J_S5, FF_TILE)
            up_scr[chunks] = up
            if is_ctx:
                continue
            for c in range(RG // CHUNK):
                m = rg * (RG // CHUNK) + c
                for s in range(T_S5):
                    turned = pltpu.roll(up[c, s], half, 0)
                    sh_scr[m, s, half:J_S5, :] = turned[half:J_S5]
                    sh_scr[m + 1, s, 0:half, :] = turned[0:half]

    def run_loop(parity, do_up, do_conv):
        def body(rg, carry):
            if do_conv:
                stage_conv(rg, *bufs[1 - parity])
            if do_up:
                stage_up(rg, *bufs[parity])
            return carry
        lax.fori_loop(0, N_RG, body, 0)

    def down(final):
        for rg in range(N_RG):
            rows = slice(rg * RG, (rg + 1) * RG)
            d = jnp.dot(act_scr[rows, :], wd_ref[...], preferred_element_type=F32)
            if final:
                o_ref[0, rows, :] = x_ref[0, rows, :] + mod_ref[0, 5:6, :] * (o_ref[0, rows, :] + d)
            else:
                o_ref[0, rows, :] += d

    steady = jnp.logical_and(jf >= 1, jf < N_FF)
    even = lax.rem(jf, 2) == 0
    pl.when(jf == 0)(lambda: run_loop(0, True, False))
    pl.when(jnp.logical_and(steady, even))(lambda: run_loop(0, True, True))
    pl.when(jnp.logical_and(steady, jnp.logical_not(even)))(lambda: run_loop(1, True, True))
    pl.when(jf == N_FF)(lambda: run_loop(N_FF % 2, False, True))
    pl.when(steady)(lambda: down(False))
    pl.when(jf == N_FF)(lambda: down(True))


def _ffn(x, mods, g_ffn, w_up, w_conv, w_down, *, layer, is_ctx):
    nblk = x.shape[0]
    last = N_FF - 1
    up_tile = lambda j: jnp.minimum(j, last)
    prev_tile = lambda j: jnp.maximum(j - 1, 0)
    up_shape = (M_BLK, T_S5, J_S5, FF_TILE)
    sh_shape = (1, 1, J_S5, LANE) if is_ctx else (M_BLK + 1, T_S5, J_S5, FF_TILE)
    buf_set = [pltpu.VMEM(up_shape, F32), pltpu.VMEM(up_shape, F32),
               pltpu.VMEM(sh_shape, F32), pltpu.VMEM(sh_shape, F32)]
    return pl.pallas_call(
        functools.partial(_ffn_kernel, is_ctx=is_ctx),
        grid=(nblk, N_FF + 1),
        in_specs=[
            pl.BlockSpec((1, BLK, D_MODEL), lambda b, j: (b, 0, 0), pipeline_mode=pl.Buffered(1)),
            _mod_spec(layer, is_ctx),
            _layer_spec((1, D_MODEL), layer),
            pl.BlockSpec((None, D_MODEL, FF_TILE), lambda b, j: (layer, 0, up_tile(j))),
            pl.BlockSpec((None, D_MODEL, FF_TILE), lambda b, j: (layer, 0, N_FF + up_tile(j))),
            pl.BlockSpec((None, 9, FF_TILE), lambda b, j: (layer, 0, prev_tile(j))),
            pl.BlockSpec((None, 9, FF_TILE), lambda b, j: (layer, 0, N_FF + prev_tile(j))),
            pl.BlockSpec((None, FF_TILE, D_MODEL), lambda b, j: (layer, prev_tile(j), 0)),
        ],
        out_specs=pl.BlockSpec((1, BLK, D_MODEL), lambda b, j: (b, 0, 0)),
        out_shape=jax.ShapeDtypeStruct(x.shape, F32),
        scratch_shapes=[pltpu.VMEM((BLK, D_MODEL), BF16), pltpu.VMEM((BLK, FF_TILE), BF16)]
        + 2 * buf_set,
        compiler_params=pltpu.CompilerParams(
            dimension_semantics=("arbitrary", "arbitrary"), vmem_limit_bytes=VMEM_LIMIT),
        name="ffn_ctx" if is_ctx else "ffn",
    )(x, mods, g_ffn, w_up, w_up, w_conv, w_conv, w_down)


def _final_kernel(x_ref, g_ref, o_ref):
    o_ref[0] = _rms(x_ref[0]) * g_ref[...]


def _final_norm(x, g):
    nblk = x.shape[0]
    return pl.pallas_call(
        _final_kernel,
        grid=(nblk, N_SUB),
        in_specs=[pl.BlockSpec((1, SUB, D_MODEL), lambda b, j: (b, j, 0)),
                  pl.BlockSpec((1, D_MODEL), lambda b, j: (0, 0))],
        out_specs=pl.BlockSpec((1, SUB, D_MODEL), lambda b, j: (b, j, 0)),
        out_shape=jax.ShapeDtypeStruct(x.shape, F32),
        name="final_norm",
    )(x, g)


def _s5_operators(a_re, a_im, b_re, b_im, c_re, c_im, log_dt, d_skip):
    hp = lax.Precision.HIGHEST
    mat = (DEPTH, SSM_GROUPS, S5_MAT, S5_MAT)
    dt = jnp.exp(log_dt)[..., None, None]
    adt_re, adt_im = a_re[..., None] * dt, a_im[..., None] * dt
    k = jnp.arange(T_S5 + 1, dtype=F32)
    mag = jnp.exp(adt_re * k)
    pw_re, pw_im = mag * jnp.cos(adt_im * k), mag * jnp.sin(adt_im * k)
    num_re, num_im = pw_re[..., 1] - 1.0, pw_im[..., 1]
    den = a_re * a_re + a_im * a_im
    cf_re = (num_re * a_re + num_im * a_im) / den
    cf_im = (num_im * a_re - num_re * a_im) / den
    bb_re = cf_re[..., None] * b_re - cf_im[..., None] * b_im
    bb_im = cf_re[..., None] * b_im + cf_im[..., None] * b_re
    width = S5_MAT

    def zflat(direction, powers):
        pr, pi = pw_re[:, direction, ..., powers, None], pw_im[:, direction, ..., powers, None]
        br, bi = bb_re[:, direction, ..., None, :], bb_im[:, direction, ..., None, :]
        shape = pr.shape[:3] + (-1,)
        return (pr * br - pi * bi).reshape(shape), (pr * bi + pi * br).reshape(shape)

    def lag_kernels(direction, z):
        return (jnp.einsum("egcp,egpx->egcx", c_re[:, direction], z[0], precision=hp)
                - jnp.einsum("egcp,egpx->egcx", c_im[:, direction], z[1], precision=hp))

    zf = zflat(0, slice(T_S5, None, -1))
    zb = zflat(1, slice(0, T_S5 + 1))
    kf, kb = lag_kernels(0, zf), lag_kernels(1, zb)
    skip = d_skip.reshape(DEPTH, SSM_GROUPS, SSM_CH)[..., None] * jnp.eye(SSM_CH, dtype=F32)
    centre = kf[..., width:width + SSM_CH] + kb[..., :SSM_CH] + skip
    band = jnp.concatenate([kf[..., SSM_CH:width], centre, kb[..., SSM_CH:width]], axis=-1)
    mt = jnp.stack([band[..., (T_S5 - 1 - t) * SSM_CH:(T_S5 - 1 - t) * SSM_CH + width]
                    for t in range(T_S5)], axis=2).reshape(mat)

    pt = jnp.stack([zf[0][..., SSM_CH:], zf[1][..., SSM_CH:], zb[0][..., :width], zb[1][..., :width]],
                   axis=2).reshape(mat)

    def cq(direction, powers):
        cr, ci = c_re[:, direction, :, None], c_im[:, direction, :, None]
        pr = jnp.moveaxis(pw_re[:, direction], -1, 2)[:, :, powers, None, :]
        pi = jnp.moveaxis(pw_im[:, direction], -1, 2)[:, :, powers, None, :]
        return cr * pr - ci * pi, cr * pi + ci * pr

    qf_re, qf_im = cq(0, slice(1, T_S5 + 1))
    qb_re, qb_im = cq(1, slice(T_S5, 0, -1))
    qt = jnp.concatenate([qf_re, -qf_im, qb_re, -qb_im], axis=-1).reshape(mat)

    a16 = jnp.stack([pw_re[:, 0, ..., T_S5], pw_im[:, 0, ..., T_S5],
                     pw_re[:, 1, ..., T_S5], pw_im[:, 1, ..., T_S5]], axis=1)
    return mt.astype(BF16), pt.astype(BF16), qt.astype(BF16), a16.reshape(DEPTH, 4, N_STATE)


def _chunk_perm():
    i = jnp.arange(CHUNK)
    return 16 * (i % J_S5) + i // J_S5


def _to_blocks(x, chunks):
    b = x.shape[0]
    x = x.reshape(b, chunks, J_S5, T_S5, D_MODEL).transpose(0, 1, 3, 2, 4)
    return x.reshape(b, chunks * CHUNK, D_MODEL)


def _from_blocks(x, chunks):
    b = x.shape[0]
    x = x.reshape(b, chunks, T_S5, J_S5, D_MODEL).transpose(0, 1, 3, 2, 4)
    return x.reshape(b, chunks * CHUNK, D_MODEL)


def kernel(x, c, ctx, c_ctx, w_ada, b_ada, g_mix, w_in, ssm_a_re, ssm_a_im, ssm_b_re, ssm_b_im,
           ssm_c_re, ssm_c_im, ssm_log_dt, ssm_d, w_glu, b_glu, g_sgu, w_spatial, b_spatial,
           w_out, g_ffn, w_up, w_conv, w_down, g_final):
    xl = _to_blocks(x, SEQ // CHUNK)
    xc = _to_blocks(ctx, CTX_LEN // CHUNK).reshape(1, BLK, D_MODEL)

    cond = jnp.concatenate([c_ctx[None], c, jnp.zeros((7, D_MODEL), F32)], axis=0)
    mods = _modulation(cond, w_ada, b_ada).reshape(DEPTH, 16, 6, D_MODEL)

    mt, pt, qt, avec = _s5_operators(ssm_a_re, ssm_a_im, ssm_b_re, ssm_b_im, ssm_c_re, ssm_c_im,
                                     ssm_log_dt, ssm_d)
    perm = _chunk_perm()
    ws = w_spatial[:, :, perm][:, :, :, perm].astype(BF16)
    bs = b_spatial[:, :, perm][..., None]
    w_in_b, w_glu_b, w_out_b = w_in.astype(BF16), w_glu.astype(BF16), w_out.astype(BF16)
    w_up_b, w_down_b = w_up.astype(BF16), w_down.astype(BF16)
    w_conv9 = w_conv.reshape(DEPTH, 9, 2 * D_FF)
    gm, gs, gf = g_mix[:, None], g_sgu[:, None], g_ffn[:, None]
    bg = b_glu[:, None]
    zero_h0 = jnp.zeros((1, 4, N_STATE), F32)

    for i in range(DEPTH):
        need_ctx = i < DEPTH - 1
        r_c, sgu_c = _mixer_in(xc, mods, gm, w_in_b, gs, ws, bs, layer=i, is_ctx=True,
                               with_sgu=need_ctx)
        y_c, hfin = _s5_core(r_c, mt, pt, qt, avec, zero_h0, layer=i, is_ctx=True, with_y=need_ctx)
        h0 = hfin.transpose(1, 0, 2)

        r_l, sgu_l = _mixer_in(xl, mods, gm, w_in_b, gs, ws, bs, layer=i, is_ctx=False,
                               with_sgu=True)
        y_l, _ = _s5_core(r_l, mt, pt, qt, avec, h0, layer=i, is_ctx=False, with_y=True)
        xl = _mixer_out(xl, y_l, sgu_l, mods, w_glu_b, bg, w_out_b, layer=i, is_ctx=False)
        xl = _ffn(xl, mods, gf, w_up_b, w_conv9, w_down_b, layer=i, is_ctx=False)

        if need_ctx:
            xc = _mixer_out(xc, y_c, sgu_c, mods, w_glu_b, bg, w_out_b, layer=i, is_ctx=True)
            xc = _ffn(xc, mods, gf, w_up_b, w_conv9, w_down_b, layer=i, is_ctx=True)

    out = _final_norm(xl, g_final[None])
    return _from_blocks(out, SEQ // CHUNK)
```

```python
import functools

import jax
import jax.numpy as jnp
from jax import lax
from jax.experimental import pallas as pl
from jax.experimental.pallas import tpu as pltpu

D_MODEL = 1024
BATCH = 8
SEQ = 2048
DEPTH = 4
CTX_LEN = 256
D_SSM = 512
SSM_CH = 16
SSM_GROUPS = 32
SSM_STATE = 64
D_SGU = 512
SGU_HEADS = 4
SGU_HEAD_DIM = 128
CHUNK = 128
D_FF = 2816
EPS = 1e-6
D_IN = D_SSM + 2 * D_SGU

T_S5 = 16
J_S5 = CHUNK // T_S5
BLK = 2048
M_BLK = BLK // CHUNK
N_COL = BLK // T_S5
N_STATE = SSM_GROUPS * SSM_STATE
S5_MAT = T_S5 * SSM_CH
SUB = 512
N_SUB = BLK // SUB
FF_TILE = 256
N_FF = D_FF // FF_TILE
RG = 2048
N_RG = BLK // RG
LANE = 128
VMEM_LIMIT = 56 * 1024 * 1024

F32 = jnp.float32
BF16 = jnp.bfloat16


def _rms(x):
    return x * lax.rsqrt(jnp.mean(x * x, axis=-1, keepdims=True) + EPS)


def _layer_spec(shape, layer, **kw):
    zeros = (0,) * len(shape)
    return pl.BlockSpec((None,) + tuple(shape), lambda *_: (layer,) + zeros, **kw)


def _mod_spec(layer, is_ctx):
    if is_ctx:
        return pl.BlockSpec((None, 1, 6, D_MODEL), lambda b, *_: (layer, 0, 0, 0))
    return pl.BlockSpec((None, 1, 6, D_MODEL), lambda b, *_: (layer, 1 + b, 0, 0))


def _mod_kernel(c_ref, w_ref, b_ref, o_ref):
    cond = jax.nn.silu(c_ref[...]).astype(BF16)
    o_ref[0] = jnp.dot(cond, w_ref[0].astype(BF16), preferred_element_type=F32) + b_ref[0]


def _modulation(cond, w_ada, b_ada):
    rows = cond.shape[0]
    n_out = w_ada.shape[-1] // D_MODEL
    return pl.pallas_call(
        _mod_kernel,
        grid=(DEPTH, n_out),
        in_specs=[
            pl.BlockSpec((rows, D_MODEL), lambda i, j: (0, 0)),
            pl.BlockSpec((1, D_MODEL, D_MODEL), lambda i, j: (i, 0, j)),
            pl.BlockSpec((1, 1, D_MODEL), lambda i, j: (i, 0, j)),
        ],
        out_specs=pl.BlockSpec((1, rows, D_MODEL), lambda i, j: (i, 0, j)),
        out_shape=jax.ShapeDtypeStruct((DEPTH, rows, n_out * D_MODEL), F32),
        name="adaln_mod",
    )(cond, w_ada, b_ada.reshape(DEPTH, 1, -1))


def _mixer_in_kernel(x_ref, mod_ref, g_ref, win_ref, gsgu_ref, ws_ref, bs_ref,
                     r_ref, sgu_ref, u_scr, *, with_sgu):
    j = pl.program_id(1)
    xn = _rms(x_ref[0]) * g_ref[...]
    hm = (xn * (1.0 + mod_ref[0, 1:2, :]) + mod_ref[0, 0:1, :]).astype(BF16)
    if with_sgu:
        p = jnp.dot(hm, win_ref[...], preferred_element_type=F32)
    else:
        p = jnp.dot(hm, win_ref[:, :D_SSM], preferred_element_type=F32)
    u_scr[pl.ds(j * (SUB // CHUNK), SUB // CHUNK)] = (
        p[:, :D_SSM].reshape(SUB // CHUNK, T_S5, J_S5, D_SSM))
    if with_sgu:
        ug = jax.nn.gelu(p[:, D_SSM:D_SSM + D_SGU])
        vn = (_rms(jax.nn.gelu(p[:, D_SSM + D_SGU:])) * gsgu_ref[...]).astype(BF16)
        for c in range(SUB // CHUNK):
            rows = slice(c * CHUNK, (c + 1) * CHUNK)
            for h in range(SGU_HEADS):
                cols = slice(h * SGU_HEAD_DIM, (h + 1) * SGU_HEAD_DIM)
                mixed = jnp.dot(ws_ref[h], vn[rows, cols], preferred_element_type=F32) + bs_ref[h]
                sgu_ref[0, rows, cols] = (ug[rows, cols] * mixed).astype(BF16)

    @pl.when(j == N_SUB - 1)
    def _():
        for s in range(T_S5):
            us = u_scr[:, s, :, :].reshape(N_COL, D_SSM)
            r_ref[0, :, s * SSM_CH:(s + 1) * SSM_CH, :] = (
                us.T.astype(BF16).reshape(SSM_GROUPS, SSM_CH, N_COL))


def _mixer_in_kernel_ssm_only(x_ref, mod_ref, g_ref, win_ref, gsgu_ref, ws_ref, bs_ref,
                              r_ref, u_scr):
    _mixer_in_kernel(x_ref, mod_ref, g_ref, win_ref, gsgu_ref, ws_ref, bs_ref,
                     r_ref, None, u_scr, with_sgu=False)


def _mixer_in(x, mods, g_mix, w_in, g_sgu, ws, bs, *, layer, is_ctx, with_sgu):
    nblk = x.shape[0]
    out_shape = [jax.ShapeDtypeStruct((nblk, SSM_GROUPS, S5_MAT, N_COL), BF16)]
    out_specs = [pl.BlockSpec((1, SSM_GROUPS, S5_MAT, N_COL), lambda b, j: (b, 0, 0, 0))]
    if with_sgu:
        out_shape.append(jax.ShapeDtypeStruct((nblk, BLK, D_SGU), BF16))
        out_specs.append(pl.BlockSpec((1, SUB, D_SGU), lambda b, j: (b, j, 0)))
        kern = functools.partial(_mixer_in_kernel, with_sgu=True)
    else:
        kern = _mixer_in_kernel_ssm_only
    res = pl.pallas_call(
        kern,
        grid=(nblk, N_SUB),
        in_specs=[
            pl.BlockSpec((1, SUB, D_MODEL), lambda b, j: (b, j, 0)),
            _mod_spec(layer, is_ctx),
            _layer_spec((1, D_MODEL), layer),
            _layer_spec((D_MODEL, D_IN), layer),
            _layer_spec((1, D_SGU), layer),
            _layer_spec((SGU_HEADS, CHUNK, CHUNK), layer),
            _layer_spec((SGU_HEADS, CHUNK, 1), layer),
        ],
        out_specs=out_specs,
        out_shape=out_shape,
        scratch_shapes=[pltpu.VMEM((M_BLK, T_S5, J_S5, D_SSM), F32)],
        compiler_params=pltpu.CompilerParams(
            dimension_semantics=("arbitrary", "arbitrary"), vmem_limit_bytes=VMEM_LIMIT),
        name="mixer_in",
    )(x, mods, g_mix, w_in, g_sgu, ws, bs)
    return res if with_sgu else (res[0], None)


SCAN_LANES = 512


def _s5_kernel(r_ref, mt_ref, pt_ref, qt_ref, a_ref, h0_ref, y_ref, hfin_ref,
               yt_scr, et_scr, e_scr, *, is_ctx, with_y):
    kinds = 4
    pstate = SSM_STATE

    def chunk_mm(g, carry):
        rg = r_ref[0, g]
        if with_y:
            yt_scr[g] = jnp.dot(mt_ref[g], rg, preferred_element_type=F32)
        et_scr[g] = jnp.dot(pt_ref[g], rg, preferred_element_type=F32)
        return carry

    lax.fori_loop(0, SSM_GROUPS, chunk_mm, 0, unroll=4)

    for k in range(kinds):
        blk = et_scr[:, k * pstate:(k + 1) * pstate, :].reshape(N_STATE, N_COL)
        e_scr[k] = blk.T

    per_seq = CTX_LEN // T_S5
    for lb in range(N_STATE // SCAN_LANES):
        lanes = slice(lb * SCAN_LANES, (lb + 1) * SCAN_LANES)
        af_re, af_im = a_ref[0:1, lanes], a_ref[1:2, lanes]
        ab_re, ab_im = a_ref[2:3, lanes], a_ref[3:4, lanes]
        if is_ctx:
            init = tuple(jnp.zeros((1, SCAN_LANES), F32) for _ in range(kinds))
        else:
            init = tuple(h0_ref[0, k:k + 1, lanes] for k in range(kinds))

        def step(i, carry, lanes=lanes, af_re=af_re, af_im=af_im, ab_re=ab_re, ab_im=ab_im):
            f_re, f_im, b_re, b_im = carry
            cf = i
            cb = N_COL - 1 - i
            if is_ctx:
                first = lax.rem(cf, per_seq) == 0
                f_re = jnp.where(first, 0.0, f_re)
                f_im = jnp.where(first, 0.0, f_im)
                b_re = jnp.where(first, 0.0, b_re)
                b_im = jnp.where(first, 0.0, b_im)
            ef_re = e_scr[0, pl.ds(cf, 1), lanes]
            ef_im = e_scr[1, pl.ds(cf, 1), lanes]
            eb_re = e_scr[2, pl.ds(cb, 1), lanes]
            eb_im = e_scr[3, pl.ds(cb, 1), lanes]
            e_scr[0, pl.ds(cf, 1), lanes] = f_re
            e_scr[1, pl.ds(cf, 1), lanes] = f_im
            e_scr[2, pl.ds(cb, 1), lanes] = b_re
            e_scr[3, pl.ds(cb, 1), lanes] = b_im
            nf_re = af_re * f_re - af_im * f_im + ef_re
            nf_im = af_re * f_im + af_im * f_re + ef_im
            nb_re = ab_re * b_re - ab_im * b_im + eb_re
            nb_im = ab_re * b_im + ab_im * b_re + eb_im
            if is_ctx:
                hfin_ref[0, pl.ds(cf // per_seq, 1), lanes] = nf_re
                hfin_ref[1, pl.ds(cf // per_seq, 1), lanes] = nf_im
                hfin_ref[2, pl.ds(cb // per_seq, 1), lanes] = nb_re
                hfin_ref[3, pl.ds(cb // per_seq, 1), lanes] = nb_im
            return nf_re, nf_im, nb_re, nb_im

        lax.fori_loop(0, N_COL, step, init)

    if with_y:
        for k in range(kinds):
            et_scr[:, k * pstate:(k + 1) * pstate, :] = (
                e_scr[k].T.reshape(SSM_GROUPS, pstate, N_COL))

        def carry_mm(g, carry):
            yt_scr[g] += jnp.dot(qt_ref[g], et_scr[g].astype(BF16), preferred_element_type=F32)
            return carry

        lax.fori_loop(0, SSM_GROUPS, carry_mm, 0, unroll=4)

        for t in range(T_S5):
            slab = yt_scr[:, t * SSM_CH:(t + 1) * SSM_CH, :].reshape(D_SSM, N_COL)
            y_ref[0, :, t, :, :] = slab.T.reshape(M_BLK, J_S5, D_SSM)


def _s5_core(r, mt, pt, qt, avec, h0, *, layer, is_ctx, with_y):
    nblk = r.shape[0]
    wspec = _layer_spec((SSM_GROUPS, S5_MAT, S5_MAT), layer, pipeline_mode=pl.Buffered(1))
    out_shape, out_specs = [], []
    if with_y:
        out_shape.append(jax.ShapeDtypeStruct((nblk, M_BLK, T_S5, J_S5, D_SSM), F32))
        out_specs.append(pl.BlockSpec((1, M_BLK, T_S5, J_S5, D_SSM), lambda b: (b, 0, 0, 0, 0)))
    if is_ctx:
        out_shape.append(jax.ShapeDtypeStruct((4, BATCH, N_STATE), F32))
        out_specs.append(pl.BlockSpec((4, BATCH, N_STATE), lambda b: (0, 0, 0)))

    def kern(r_ref, mt_ref, pt_ref, qt_ref, a_ref, h0_ref, *rest):
        outs, scr = rest[:len(out_shape)], rest[len(out_shape):]
        y_ref = outs[0] if with_y else None
        hfin_ref = outs[-1] if is_ctx else None
        _s5_kernel(r_ref, mt_ref, pt_ref, qt_ref, a_ref, h0_ref, y_ref, hfin_ref, *scr,
                   is_ctx=is_ctx, with_y=with_y)

    res = pl.pallas_call(
        kern,
        grid=(nblk,),
        in_specs=[
            pl.BlockSpec((1, SSM_GROUPS, S5_MAT, N_COL), lambda b: (b, 0, 0, 0)),
            wspec, wspec, wspec,
            _layer_spec((4, N_STATE), layer),
            pl.BlockSpec((1, 4, N_STATE), lambda b: (b, 0, 0)),
        ],
        out_specs=out_specs,
        out_shape=out_shape,
        scratch_shapes=[
            pltpu.VMEM((SSM_GROUPS, S5_MAT, N_COL), F32),
            pltpu.VMEM((SSM_GROUPS, S5_MAT, N_COL), F32),
            pltpu.VMEM((4, N_COL, N_STATE), F32),
        ],
        compiler_params=pltpu.CompilerParams(
            dimension_semantics=("arbitrary",), vmem_limit_bytes=VMEM_LIMIT),
        name="s5_core_ctx" if is_ctx else "s5_core",
    )(r, mt, pt, qt, avec, h0)
    y = res[0].reshape(nblk, BLK, D_SSM) if with_y else None
    hfin = res[-1] if is_ctx else None
    return y, hfin


def _mixer_out_kernel(x_ref, y_ref, sgu_ref, mod_ref, wglu_ref, bglu_ref, wout_ref, o_ref):
    yg = jax.nn.gelu(y_ref[0])
    z = jnp.dot(yg.astype(BF16), wglu_ref[...], preferred_element_type=F32) + bglu_ref[...]
    glu = (yg * jax.nn.sigmoid(z)).astype(BF16)
    delta = jnp.dot(glu, wout_ref[:D_SSM, :], preferred_element_type=F32)
    delta += jnp.dot(sgu_ref[0], wout_ref[D_SSM:, :], preferred_element_type=F32)
    o_ref[0] = x_ref[0] + mod_ref[0, 2:3, :] * delta


def _mixer_out(x, y, sgu, mods, w_glu, b_glu, w_out, *, layer, is_ctx):
    nblk = x.shape[0]
    return pl.pallas_call(
        _mixer_out_kernel,
        grid=(nblk, N_SUB),
        in_specs=[
            pl.BlockSpec((1, SUB, D_MODEL), lambda b, j: (b, j, 0)),
            pl.BlockSpec((1, SUB, D_SSM), lambda b, j: (b, j, 0)),
            pl.BlockSpec((1, SUB, D_SGU), lambda b, j: (b, j, 0)),
            _mod_spec(layer, is_ctx),
            _layer_spec((D_SSM, D_SSM), layer),
            _layer_spec((1, D_SSM), layer),
            _layer_spec((D_MODEL, D_MODEL), layer),
        ],
        out_specs=pl.BlockSpec((1, SUB, D_MODEL), lambda b, j: (b, j, 0)),
        out_shape=jax.ShapeDtypeStruct(x.shape, F32),
        input_output_aliases={0: 0},
        compiler_params=pltpu.CompilerParams(
            dimension_semantics=("arbitrary", "arbitrary"), vmem_limit_bytes=VMEM_LIMIT),
        name="mixer_out",
    )(x, y, sgu, mods, w_glu, b_glu, w_out)


def _conv_tiles(up_scr, sh_scr, w_ref, m, lanes, *, is_ctx):
    sub = lax.broadcasted_iota(jnp.int32, (J_S5, LANE), 0)
    w = [jnp.broadcast_to(w_ref[k:k + 1, lanes], (J_S5, LANE)) for k in range(9)]

    def taps(mm, s):
        x = up_scr[mm, s, :, lanes]
        if is_ctx:
            return [w[3 + dw] * x for dw in range(3)]
        above = sh_scr[mm, s, :, lanes]
        below = sh_scr[mm + 1, s, :, lanes]
        return [w[dw] * above + w[3 + dw] * x + w[6 + dw] * below for dw in range(3)]

    v = [taps(m, s) for s in range(T_S5)]
    out = []
    for s in range(T_S5):
        acc = v[s][1]
        if s > 0:
            acc = acc + v[s - 1][0]
        else:
            edge = pltpu.roll(v[T_S5 - 1][0], 1, 0)
            if is_ctx:
                prev = pltpu.roll(taps(jnp.maximum(m - 1, 0), T_S5 - 1)[0], 1, 0)
                prev = jnp.where(lax.rem(m, 2) == 1, prev, 0.0)
                edge = jnp.where(sub == 0, prev, edge)
            else:
                edge = jnp.where(lax.rem(sub, 4) == 0, 0.0, edge)
            acc = acc + edge
        if s < T_S5 - 1:
            acc = acc + v[s + 1][2]
        else:
            edge = pltpu.roll(v[0][2], J_S5 - 1, 0)
            if is_ctx:
                nxt = pltpu.roll(taps(jnp.minimum(m + 1, M_BLK - 1), 0)[2], J_S5 - 1, 0)
                nxt = jnp.where(lax.rem(m, 2) == 0, nxt, 0.0)
                edge = jnp.where(sub == J_S5 - 1, nxt, edge)
            else:
                edge = jnp.where(lax.rem(sub, 4) == 3, 0.0, edge)
            acc = acc + edge
        out.append(acc)
    return out


def _ffn_kernel(x_ref, mod_ref, g_ref, wg_ref, wv_ref, cg_ref, cv_ref, wd_ref, o_ref,
                hf_scr, act_scr, *up_scrs, is_ctx):
    jf = pl.program_id(1)
    bufs = (up_scrs[0:4], up_scrs[4:8])
    half = J_S5 // 2

    @pl.when(jf == 0)
    def _():
        def norm(i, carry):
            rows = pl.ds(pl.multiple_of(i * SUB, SUB), SUB)
            xn = _rms(x_ref[0, rows, :]) * g_ref[...]
            hf_scr[rows, :] = (xn * (1.0 + mod_ref[0, 4:5, :]) + mod_ref[0, 3:4, :]).astype(BF16)
            o_ref[0, rows, :] = jnp.zeros((SUB, D_MODEL), F32)
            return carry
        lax.fori_loop(0, N_SUB, norm, 0)
        if not is_ctx:
            for bset in bufs:
                for sh_scr in bset[2:]:
                    sh_scr[0, :, 0:half, :] = jnp.zeros((T_S5, half, FF_TILE), F32)
                    sh_scr[M_BLK, :, half:J_S5, :] = jnp.zeros((T_S5, half, FF_TILE), F32)

    def stage_conv(rg, upg_scr, upv_scr, shg_scr, shv_scr):
        for c in range(RG // CHUNK):
            m = rg * (RG // CHUNK) + c
            for cb in range(FF_TILE // LANE):
                lanes = slice(cb * LANE, (cb + 1) * LANE)
                gate = _conv_tiles(upg_scr, shg_scr, cg_ref, m, lanes, is_ctx=is_ctx)
                val = _conv_tiles(upv_scr, shv_scr, cv_ref, m, lanes, is_ctx=is_ctx)
                act = [jax.nn.silu(g) * v for g, v in zip(gate, val)]
                for k in range(T_S5 // 2):
                    pair = jnp.concatenate([act[2 * k], act[2 * k + 1]], axis=0).astype(BF16)
                    row0 = pl.multiple_of(m * CHUNK + 2 * k * J_S5, 2 * J_S5)
                    act_scr[pl.ds(row0, 2 * J_S5), lanes] = pair

    def stage_up(rg, upg_scr, upv_scr, shg_scr, shv_scr):
        rows = pl.ds(pl.multiple_of(rg * RG, RG), RG)
        chunks = pl.ds(rg * (RG // CHUNK), RG // CHUNK)
        hf = hf_scr[rows, :]
        for w_ref, up_scr, sh_scr in ((wg_ref, upg_scr, shg_scr), (wv_ref, upv_scr, shv_scr)):
            up = jnp.dot(hf, w_ref[...], preferred_element_type=F32).reshape(
                RG // CHUNK, T_S5, J_S5, FF_TILE)
            up_scr[chunks] = up
            if is_ctx:
                continue
            for c in range(RG // CHUNK):
                m = rg * (RG // CHUNK) + c
                for s in range(T_S5):
                    turned = pltpu.roll(up[c, s], half, 0)
                    sh_scr[m, s, half:J_S5, :] = turned[half:J_S5]
                    sh_scr[m + 1, s, 0:half, :] = turned[0:half]

    def run_loop(parity, do_up, do_conv):
        def body(rg, carry):
            if do_conv:
                stage_conv(rg, *bufs[1 - parity])
            if do_up:
                stage_up(rg, *bufs[parity])
            return carry
        lax.fori_loop(0, N_RG, body, 0)

    def down(final):
        for rg in range(N_RG):
            rows = slice(rg * RG, (rg + 1) * RG)
            d = jnp.dot(act_scr[rows, :], wd_ref[...], preferred_element_type=F32)
            if final:
                o_ref[0, rows, :] = x_ref[0, rows, :] + mod_ref[0, 5:6, :] * (o_ref[0, rows, :] + d)
            else:
                o_ref[0, rows, :] += d

    steady = jnp.logical_and(jf >= 1, jf < N_FF)
    even = lax.rem(jf, 2) == 0
    pl.when(jf == 0)(lambda: run_loop(0, True, False))
    pl.when(jnp.logical_and(steady, even))(lambda: run_loop(0, True, True))
    pl.when(jnp.logical_and(steady, jnp.logical_not(even)))(lambda: run_loop(1, True, True))
    pl.when(jf == N_FF)(lambda: run_loop(N_FF % 2, False, True))
    pl.when(steady)(lambda: down(False))
    pl.when(jf == N_FF)(lambda: down(True))


def _ffn(x, mods, g_ffn, w_up, w_conv, w_down, *, layer, is_ctx):
    nblk = x.shape[0]
    last = N_FF - 1
    up_tile = lambda j: jnp.minimum(j, last)
    prev_tile = lambda j: jnp.maximum(j - 1, 0)
    up_shape = (M_BLK, T_S5, J_S5, FF_TILE)
    sh_shape = (1, 1, J_S5, LANE) if is_ctx else (M_BLK + 1, T_S5, J_S5, FF_TILE)
    buf_set = [pltpu.VMEM(up_shape, F32), pltpu.VMEM(up_shape, F32),
               pltpu.VMEM(sh_shape, F32), pltpu.VMEM(sh_shape, F32)]
    return pl.pallas_call(
        functools.partial(_ffn_kernel, is_ctx=is_ctx),
        grid=(nblk, N_FF + 1),
        in_specs=[
            pl.BlockSpec((1, BLK, D_MODEL), lambda b, j: (b, 0, 0), pipeline_mode=pl.Buffered(1)),
            _mod_spec(layer, is_ctx),
            _layer_spec((1, D_MODEL), layer),
            pl.BlockSpec((None, D_MODEL, FF_TILE), lambda b, j: (layer, 0, up_tile(j))),
            pl.BlockSpec((None, D_MODEL, FF_TILE), lambda b, j: (layer, 0, N_FF + up_tile(j))),
            pl.BlockSpec((None, 9, FF_TILE), lambda b, j: (layer, 0, prev_tile(j))),
            pl.BlockSpec((None, 9, FF_TILE), lambda b, j: (layer, 0, N_FF + prev_tile(j))),
            pl.BlockSpec((None, FF_TILE, D_MODEL), lambda b, j: (layer, prev_tile(j), 0)),
        ],
        out_specs=pl.BlockSpec((1, BLK, D_MODEL), lambda b, j: (b, 0, 0)),
        out_shape=jax.ShapeDtypeStruct(x.shape, F32),
        scratch_shapes=[pltpu.VMEM((BLK, D_MODEL), BF16), pltpu.VMEM((BLK, FF_TILE), BF16)]
        + 2 * buf_set,
        compiler_params=pltpu.CompilerParams(
            dimension_semantics=("arbitrary", "arbitrary"), vmem_limit_bytes=VMEM_LIMIT),
        name="ffn_ctx" if is_ctx else "ffn",
    )(x, mods, g_ffn, w_up, w_up, w_conv, w_conv, w_down)


def _final_kernel(x_ref, g_ref, o_ref):
    o_ref[0] = _rms(x_ref[0]) * g_ref[...]


def _final_norm(x, g):
    nblk = x.shape[0]
    return pl.pallas_call(
        _final_kernel,
        grid=(nblk, N_SUB),
        in_specs=[pl.BlockSpec((1, SUB, D_MODEL), lambda b, j: (b, j, 0)),
                  pl.BlockSpec((1, D_MODEL), lambda b, j: (0, 0))],
        out_specs=pl.BlockSpec((1, SUB, D_MODEL), lambda b, j: (b, j, 0)),
        out_shape=jax.ShapeDtypeStruct(x.shape, F32),
        name="final_norm",
    )(x, g)


def _s5_operators(a_re, a_im, b_re, b_im, c_re, c_im, log_dt, d_skip):
    hp = lax.Precision.HIGHEST
    mat = (DEPTH, SSM_GROUPS, S5_MAT, S5_MAT)
    dt = jnp.exp(log_dt)[..., None, None]
    adt_re, adt_im = a_re[..., None] * dt, a_im[..., None] * dt
    k = jnp.arange(T_S5 + 1, dtype=F32)
    mag = jnp.exp(adt_re * k)
    pw_re, pw_im = mag * jnp.cos(adt_im * k), mag * jnp.sin(adt_im * k)
    num_re, num_im = pw_re[..., 1] - 1.0, pw_im[..., 1]
    den = a_re * a_re + a_im * a_im
    cf_re = (num_re * a_re + num_im * a_im) / den
    cf_im = (num_im * a_re - num_re * a_im) / den
    bb_re = cf_re[..., None] * b_re - cf_im[..., None] * b_im
    bb_im = cf_re[..., None] * b_im + cf_im[..., None] * b_re
    width = S5_MAT

    def zflat(direction, powers):
        pr, pi = pw_re[:, direction, ..., powers, None], pw_im[:, direction, ..., powers, None]
        br, bi = bb_re[:, direction, ..., None, :], bb_im[:, direction, ..., None, :]
        shape = pr.shape[:3] + (-1,)
        return (pr * br - pi * bi).reshape(shape), (pr * bi + pi * br).reshape(shape)

    def lag_kernels(direction, z):
        return (jnp.einsum("egcp,egpx->egcx", c_re[:, direction], z[0], precision=hp)
                - jnp.einsum("egcp,egpx->egcx", c_im[:, direction], z[1], precision=hp))

    zf = zflat(0, slice(T_S5, None, -1))
    zb = zflat(1, slice(0, T_S5 + 1))
    kf, kb = lag_kernels(0, zf), lag_kernels(1, zb)
    skip = d_skip.reshape(DEPTH, SSM_GROUPS, SSM_CH)[..., None] * jnp.eye(SSM_CH, dtype=F32)
    centre = kf[..., width:width + SSM_CH] + kb[..., :SSM_CH] + skip
    band = jnp.concatenate([kf[..., SSM_CH:width], centre, kb[..., SSM_CH:width]], axis=-1)
    mt = jnp.stack([band[..., (T_S5 - 1 - t) * SSM_CH:(T_S5 - 1 - t) * SSM_CH + width]
                    for t in range(T_S5)], axis=2).reshape(mat)

    pt = jnp.stack([zf[0][..., SSM_CH:], zf[1][..., SSM_CH:], zb[0][..., :width], zb[1][..., :width]],
                   axis=2).reshape(mat)

    def cq(direction, powers):
        cr, ci = c_re[:, direction, :, None], c_im[:, direction, :, None]
        pr = jnp.moveaxis(pw_re[:, direction], -1, 2)[:, :, powers, None, :]
        pi = jnp.moveaxis(pw_im[:, direction], -1, 2)[:, :, powers, None, :]
        return cr * pr - ci * pi, cr * pi + ci * pr

    qf_re, qf_im = cq(0, slice(1, T_S5 + 1))
    qb_re, qb_im = cq(1, slice(T_S5, 0, -1))
    qt = jnp.concatenate([qf_re, -qf_im, qb_re, -qb_im], axis=-1).reshape(mat)

    a16 = jnp.stack([pw_re[:, 0, ..., T_S5], pw_im[:, 0, ..., T_S5],
                     pw_re[:, 1, ..., T_S5], pw_im[:, 1, ..., T_S5]], axis=1)
    return mt.astype(BF16), pt.astype(BF16), qt.astype(BF16), a16.reshape(DEPTH, 4, N_STATE)


def _chunk_perm():
    i = jnp.arange(CHUNK)
    return 16 * (i % J_S5) + i // J_S5


def _to_blocks(x, chunks):
    b = x.shape[0]
    x = x.reshape(b, chunks, J_S5, T_S5, D_MODEL).transpose(0, 1, 3, 2, 4)
    return x.reshape(b, chunks * CHUNK, D_MODEL)


def _from_blocks(x, chunks):
    b = x.shape[0]
    x = x.reshape(b, chunks, T_S5, J_S5, D_MODEL).transpose(0, 1, 3, 2, 4)
    return x.reshape(b, chunks * CHUNK, D_MODEL)


def kernel(x, c, ctx, c_ctx, w_ada, b_ada, g_mix, w_in, ssm_a_re, ssm_a_im, ssm_b_re, ssm_b_im,
           ssm_c_re, ssm_c_im, ssm_log_dt, ssm_d, w_glu, b_glu, g_sgu, w_spatial, b_spatial,
           w_out, g_ffn, w_up, w_conv, w_down, g_final):
    xl = _to_blocks(x, SEQ // CHUNK)
    xc = _to_blocks(ctx, CTX_LEN // CHUNK).reshape(1, BLK, D_MODEL)

    cond = jnp.concatenate([c_ctx[None], c, jnp.zeros((7, D_MODEL), F32)], axis=0)
    mods = _modulation(cond, w_ada, b_ada).reshape(DEPTH, 16, 6, D_MODEL)

    mt, pt, qt, avec = _s5_operators(ssm_a_re, ssm_a_im, ssm_b_re, ssm_b_im, ssm_c_re, ssm_c_im,
                                     ssm_log_dt, ssm_d)
    perm = _chunk_perm()
    ws = w_spatial[:, :, perm][:, :, :, perm].astype(BF16)
    bs = b_spatial[:, :, perm][..., None]
    w_in_b, w_glu_b, w_out_b = w_in.astype(BF16), w_glu.astype(BF16), w_out.astype(BF16)
    w_up_b, w_down_b = w_up.astype(BF16), w_down.astype(BF16)
    w_conv9 = w_conv.reshape(DEPTH, 9, 2 * D_FF)
    gm, gs, gf = g_mix[:, None], g_sgu[:, None], g_ffn[:, None]
    bg = b_glu[:, None]
    zero_h0 = jnp.zeros((1, 4, N_STATE), F32)

    for i in range(DEPTH):
        need_ctx = i < DEPTH - 1
        r_c, sgu_c = _mixer_in(xc, mods, gm, w_in_b, gs, ws, bs, layer=i, is_ctx=True,
                               with_sgu=need_ctx)
        y_c, hfin = _s5_core(r_c, mt, pt, qt, avec, zero_h0, layer=i, is_ctx=True, with_y=need_ctx)
        h0 = hfin.transpose(1, 0, 2)

        r_l, sgu_l = _mixer_in(xl, mods, gm, w_in_b, gs, ws, bs, layer=i, is_ctx=False,
                               with_sgu=True)
        y_l, _ = _s5_core(r_l, mt, pt, qt, avec, h0, layer=i, is_ctx=False, with_y=True)
        xl = _mixer_out(xl, y_l, sgu_l, mods, w_glu_b, bg, w_out_b, layer=i, is_ctx=False)
        xl = _ffn(xl, mods, gf, w_up_b, w_conv9, w_down_b, layer=i, is_ctx=False)

        if need_ctx:
            xc = _mixer_out(xc, y_c, sgu_c, mods, w_glu_b, bg, w_out_b, layer=i, is_ctx=True)
            xc = _ffn(xc, mods, gf, w_up_b, w_conv9, w_down_b, layer=i, is_ctx=True)

    out = _final_norm(xl, g_final[None])
    return _from_blocks(out, SEQ // CHUNK)
```

```python
import functools

import jax
import jax.numpy as jnp
from jax import lax
from jax.experimental import pallas as pl
from jax.experimental.pallas import tpu as pltpu

D_MODEL = 1024
BATCH = 8
SEQ = 2048
DEPTH = 4
CTX_LEN = 256
D_SSM = 512
SSM_CH = 16
SSM_GROUPS = 32
SSM_STATE = 64
D_SGU = 512
SGU_HEADS = 4
SGU_HEAD_DIM = 128
CHUNK = 128
D_FF = 2816
EPS = 1e-6
D_IN = D_SSM + 2 * D_SGU

T_S5 = 16
J_S5 = CHUNK // T_S5
BLK = 2048
M_BLK = BLK // CHUNK
N_COL = BLK // T_S5
N_STATE = SSM_GROUPS * SSM_STATE
S5_MAT = T_S5 * SSM_CH
SUB = 512
N_SUB = BLK // SUB
FF_TILE = 256
N_FF = D_FF // FF_TILE
RG = 2048
N_RG = BLK // RG
LANE = 128
VMEM_LIMIT = 56 * 1024 * 1024

F32 = jnp.float32
BF16 = jnp.bfloat16


def _rms(x):
    return x * lax.rsqrt(jnp.mean(x * x, axis=-1, keepdims=True) + EPS)


def _layer_spec(shape, layer, **kw):
    zeros = (0,) * len(shape)
    return pl.BlockSpec((None,) + tuple(shape), lambda *_: (layer,) + zeros, **kw)


def _mod_spec(layer, is_ctx):
    if is_ctx:
        return pl.BlockSpec((None, 1, 6, D_MODEL), lambda b, *_: (layer, 0, 0, 0))
    return pl.BlockSpec((None, 1, 6, D_MODEL), lambda b, *_: (layer, 1 + b, 0, 0))


def _mod_kernel(c_ref, w_ref, b_ref, o_ref):
    cond = jax.nn.silu(c_ref[...]).astype(BF16)
    o_ref[0] = jnp.dot(cond, w_ref[0].astype(BF16), preferred_element_type=F32) + b_ref[0]


def _modulation(cond, w_ada, b_ada):
    rows = cond.shape[0]
    n_out = w_ada.shape[-1] // D_MODEL
    return pl.pallas_call(
        _mod_kernel,
        grid=(DEPTH, n_out),
        in_specs=[
            pl.BlockSpec((rows, D_MODEL), lambda i, j: (0, 0)),
            pl.BlockSpec((1, D_MODEL, D_MODEL), lambda i, j: (i, 0, j)),
            pl.BlockSpec((1, 1, D_MODEL), lambda i, j: (i, 0, j)),
        ],
        out_specs=pl.BlockSpec((1, rows, D_MODEL), lambda i, j: (i, 0, j)),
        out_shape=jax.ShapeDtypeStruct((DEPTH, rows, n_out * D_MODEL), F32),
        name="adaln_mod",
    )(cond, w_ada, b_ada.reshape(DEPTH, 1, -1))


def _mixer_in_kernel(x_ref, mod_ref, g_ref, win_ref, gsgu_ref, ws_ref, bs_ref,
                     r_ref, sgu_ref, u_scr, *, with_sgu):
    j = pl.program_id(1)
    xn = _rms(x_ref[0]) * g_ref[...]
    hm = (xn * (1.0 + mod_ref[0, 1:2, :]) + mod_ref[0, 0:1, :]).astype(BF16)
    if with_sgu:
        p = jnp.dot(hm, win_ref[...], preferred_element_type=F32)
    else:
        p = jnp.dot(hm, win_ref[:, :D_SSM], preferred_element_type=F32)
    u_scr[pl.ds(j * (SUB // CHUNK), SUB // CHUNK)] = (
        p[:, :D_SSM].reshape(SUB // CHUNK, T_S5, J_S5, D_SSM))
    if with_sgu:
        ug = jax.nn.gelu(p[:, D_SSM:D_SSM + D_SGU])
        vn = (_rms(jax.nn.gelu(p[:, D_SSM + D_SGU:])) * gsgu_ref[...]).astype(BF16)
        for c in range(SUB // CHUNK):
            rows = slice(c * CHUNK, (c + 1) * CHUNK)
            for h in range(SGU_HEADS):
                cols = slice(h * SGU_HEAD_DIM, (h + 1) * SGU_HEAD_DIM)
                mixed = jnp.dot(ws_ref[h], vn[rows, cols], preferred_element_type=F32) + bs_ref[h]
                sgu_ref[0, rows, cols] = (ug[rows, cols] * mixed).astype(BF16)

    @pl.when(j == N_SUB - 1)
    def _():
        for s in range(T_S5):
            us = u_scr[:, s, :, :].reshape(N_COL, D_SSM)
            r_ref[0, :, s * SSM_CH:(s + 1) * SSM_CH, :] = (
                us.T.astype(BF16).reshape(SSM_GROUPS, SSM_CH, N_COL))


def _mixer_in_kernel_ssm_only(x_ref, mod_ref, g_ref, win_ref, gsgu_ref, ws_ref, bs_ref,
                              r_ref, u_scr):
    _mixer_in_kernel(x_ref, mod_ref, g_ref, win_ref, gsgu_ref, ws_ref, bs_ref,
                     r_ref, None, u_scr, with_sgu=False)


def _mixer_in(x, mods, g_mix, w_in, g_sgu, ws, bs, *, layer, is_ctx, with_sgu):
    nblk = x.shape[0]
    out_shape = [jax.ShapeDtypeStruct((nblk, SSM_GROUPS, S5_MAT, N_COL), BF16)]
    out_specs = [pl.BlockSpec((1, SSM_GROUPS, S5_MAT, N_COL), lambda b, j: (b, 0, 0, 0))]
    if with_sgu:
        out_shape.append(jax.ShapeDtypeStruct((nblk, BLK, D_SGU), BF16))
        out_specs.append(pl.BlockSpec((1, SUB, D_SGU), lambda b, j: (b, j, 0)))
        kern = functools.partial(_mixer_in_kernel, with_sgu=True)
    else:
        kern = _mixer_in_kernel_ssm_only
    res = pl.pallas_call(
        kern,
        grid=(nblk, N_SUB),
        in_specs=[
            pl.BlockSpec((1, SUB, D_MODEL), lambda b, j: (b, j, 0)),
            _mod_spec(layer, is_ctx),
            _layer_spec((1, D_MODEL), layer),
            _layer_spec((D_MODEL, D_IN), layer),
            _layer_spec((1, D_SGU), layer),
            _layer_spec((SGU_HEADS, CHUNK, CHUNK), layer),
            _layer_spec((SGU_HEADS, CHUNK, 1), layer),
        ],
        out_specs=out_specs,
        out_shape=out_shape,
        scratch_shapes=[pltpu.VMEM((M_BLK, T_S5, J_S5, D_SSM), F32)],
        compiler_params=pltpu.CompilerParams(
            dimension_semantics=("arbitrary", "arbitrary"), vmem_limit_bytes=VMEM_LIMIT),
        name="mixer_in",
    )(x, mods, g_mix, w_in, g_sgu, ws, bs)
    return res if with_sgu else (res[0], None)


SCAN_LANES = 512


def _s5_kernel(r_ref, mt_ref, pt_ref, qt_ref, a_ref, h0_ref, y_ref, hfin_ref,
               yt_scr, et_scr, e_scr, *, is_ctx, with_y):
    kinds = 4
    pstate = SSM_STATE

    def chunk_mm(g, carry):
        rg = r_ref[0, g]
        if with_y:
            yt_scr[g] = jnp.dot(mt_ref[g], rg, preferred_element_type=F32)
        et_scr[g] = jnp.dot(pt_ref[g], rg, preferred_element_type=F32)
        return carry

    lax.fori_loop(0, SSM_GROUPS, chunk_mm, 0, unroll=4)

    for k in range(kinds):
        blk = et_scr[:, k * pstate:(k + 1) * pstate, :].reshape(N_STATE, N_COL)
        e_scr[k] = blk.T

    per_seq = CTX_LEN // T_S5
    for lb in range(N_STATE // SCAN_LANES):
        lanes = slice(lb * SCAN_LANES, (lb + 1) * SCAN_LANES)
        af_re, af_im = a_ref[0:1, lanes], a_ref[1:2, lanes]
        ab_re, ab_im = a_ref[2:3, lanes], a_ref[3:4, lanes]
        if is_ctx:
            init = tuple(jnp.zeros((1, SCAN_LANES), F32) for _ in range(kinds))
        else:
            init = tuple(h0_ref[0, k:k + 1, lanes] for k in range(kinds))

        def step(i, carry, lanes=lanes, af_re=af_re, af_im=af_im, ab_re=ab_re, ab_im=ab_im):
            f_re, f_im, b_re, b_im = carry
            cf = i
            cb = N_COL - 1 - i
            if is_ctx:
                first = lax.rem(cf, per_seq) == 0
                f_re = jnp.where(first, 0.0, f_re)
                f_im = jnp.where(first, 0.0, f_im)
                b_re = jnp.where(first, 0.0, b_re)
                b_im = jnp.where(first, 0.0, b_im)
            ef_re = e_scr[0, pl.ds(cf, 1), lanes]
            ef_im = e_scr[1, pl.ds(cf, 1), lanes]
            eb_re = e_scr[2, pl.ds(cb, 1), lanes]
            eb_im = e_scr[3, pl.ds(cb, 1), lanes]
            e_scr[0, pl.ds(cf, 1), lanes] = f_re
            e_scr[1, pl.ds(cf, 1), lanes] = f_im
            e_scr[2, pl.ds(cb, 1), lanes] = b_re
            e_scr[3, pl.ds(cb, 1), lanes] = b_im
            nf_re = af_re * f_re - af_im * f_im + ef_re
            nf_im = af_re * f_im + af_im * f_re + ef_im
            nb_re = ab_re * b_re - ab_im * b_im + eb_re
            nb_im = ab_re * b_im + ab_im * b_re + eb_im
            if is_ctx:
                hfin_ref[0, pl.ds(cf // per_seq, 1), lanes] = nf_re
                hfin_ref[1, pl.ds(cf // per_seq, 1), lanes] = nf_im
                hfin_ref[2, pl.ds(cb // per_seq, 1), lanes] = nb_re
                hfin_ref[3, pl.ds(cb // per_seq, 1), lanes] = nb_im
            return nf_re, nf_im, nb_re, nb_im

        lax.fori_loop(0, N_COL, step, init)

    if with_y:
        for k in range(kinds):
            et_scr[:, k * pstate:(k + 1) * pstate, :] = (
                e_scr[k].T.reshape(SSM_GROUPS, pstate, N_COL))

        def carry_mm(g, carry):
            yt_scr[g] += jnp.dot(qt_ref[g], et_scr[g].astype(BF16), preferred_element_type=F32)
            return carry

        lax.fori_loop(0, SSM_GROUPS, carry_mm, 0, unroll=4)

        for t in range(T_S5):
            slab = yt_scr[:, t * SSM_CH:(t + 1) * SSM_CH, :].reshape(D_SSM, N_COL)
            y_ref[0, :, t, :, :] = slab.T.reshape(M_BLK, J_S5, D_SSM)


def _s5_core(r, mt, pt, qt, avec, h0, *, layer, is_ctx, with_y):
    nblk = r.shape[0]
    wspec = _layer_spec((SSM_GROUPS, S5_MAT, S5_MAT), layer, pipeline_mode=pl.Buffered(1))
    out_shape, out_specs = [], []
    if with_y:
        out_shape.append(jax.ShapeDtypeStruct((nblk, M_BLK, T_S5, J_S5, D_SSM), F32))
        out_specs.append(pl.BlockSpec((1, M_BLK, T_S5, J_S5, D_SSM), lambda b: (b, 0, 0, 0, 0)))
    if is_ctx:
        out_shape.append(jax.ShapeDtypeStruct((4, BATCH, N_STATE), F32))
        out_specs.append(pl.BlockSpec((4, BATCH, N_STATE), lambda b: (0, 0, 0)))

    def kern(r_ref, mt_ref, pt_ref, qt_ref, a_ref, h0_ref, *rest):
        outs, scr = rest[:len(out_shape)], rest[len(out_shape):]
        y_ref = outs[0] if with_y else None
        hfin_ref = outs[-1] if is_ctx else None
        _s5_kernel(r_ref, mt_ref, pt_ref, qt_ref, a_ref, h0_ref, y_ref, hfin_ref, *scr,
                   is_ctx=is_ctx, with_y=with_y)

    res = pl.pallas_call(
        kern,
        grid=(nblk,),
        in_specs=[
            pl.BlockSpec((1, SSM_GROUPS, S5_MAT, N_COL), lambda b: (b, 0, 0, 0)),
            wspec, wspec, wspec,
            _layer_spec((4, N_STATE), layer),
            pl.BlockSpec((1, 4, N_STATE), lambda b: (b, 0, 0)),
        ],
        out_specs=out_specs,
        out_shape=out_shape,
        scratch_shapes=[
            pltpu.VMEM((SSM_GROUPS, S5_MAT, N_COL), F32),
            pltpu.VMEM((SSM_GROUPS, S5_MAT, N_COL), F32),
            pltpu.VMEM((4, N_COL, N_STATE), F32),
        ],
        compiler_params=pltpu.CompilerParams(
            dimension_semantics=("arbitrary",), vmem_limit_bytes=VMEM_LIMIT),
        name="s5_core_ctx" if is_ctx else "s5_core",
    )(r, mt, pt, qt, avec, h0)
    y = res[0].reshape(nblk, BLK, D_SSM) if with_y else None
    hfin = res[-1] if is_ctx else None
    return y, hfin


def _mixer_out_kernel(x_ref, y_ref, sgu_ref, mod_ref, wglu_ref, bglu_ref, wout_ref, o_ref):
    yg = jax.nn.gelu(y_ref[0])
    z = jnp.dot(yg.astype(BF16), wglu_ref[...], preferred_element_type=F32) + bglu_ref[...]
    glu = (yg * jax.nn.sigmoid(z)).astype(BF16)
    delta = jnp.dot(glu, wout_ref[:D_SSM, :], preferred_element_type=F32)
    delta += jnp.dot(sgu_ref[0], wout_ref[D_SSM:, :], preferred_element_type=F32)
    o_ref[0] = x_ref[0] + mod_ref[0, 2:3, :] * delta


def _mixer_out(x, y, sgu, mods, w_glu, b_glu, w_out, *, layer, is_ctx):
    nblk = x.shape[0]
    return pl.pallas_call(
        _mixer_out_kernel,
        grid=(nblk, N_SUB),
        in_specs=[
            pl.BlockSpec((1, SUB, D_MODEL), lambda b, j: (b, j, 0)),
            pl.BlockSpec((1, SUB, D_SSM), lambda b, j: (b, j, 0)),
            pl.BlockSpec((1, SUB, D_SGU), lambda b, j: (b, j, 0)),
            _mod_spec(layer, is_ctx),
            _layer_spec((D_SSM, D_SSM), layer),
            _layer_spec((1, D_SSM), layer),
            _layer_spec((D_MODEL, D_MODEL), layer),
        ],
        out_specs=pl.BlockSpec((1, SUB, D_MODEL), lambda b, j: (b, j, 0)),
        out_shape=jax.ShapeDtypeStruct(x.shape, F32),
        input_output_aliases={0: 0},
        compiler_params=pltpu.CompilerParams(
            dimension_semantics=("arbitrary", "arbitrary"), vmem_limit_bytes=VMEM_LIMIT),
        name="mixer_out",
    )(x, y, sgu, mods, w_glu, b_glu, w_out)


def _conv_tiles(up_scr, sh_scr, w_ref, m, lanes, *, is_ctx):
    sub = lax.broadcasted_iota(jnp.int32, (J_S5, LANE), 0)
    w = [jnp.broadcast_to(w_ref[k:k + 1, lanes], (J_S5, LANE)) for k in range(9)]

    def taps(mm, s):
        x = up_scr[mm, s, :, lanes]
        if is_ctx:
            return [w[3 + dw] * x for dw in range(3)]
        above = sh_scr[mm, s, :, lanes]
        below = sh_scr[mm + 1, s, :, lanes]
        return [w[dw] * above + w[3 + dw] * x + w[6 + dw] * below for dw in range(3)]

    v = [taps(m, s) for s in range(T_S5)]
    out = []
    for s in range(T_S5):
        acc = v[s][1]
        if s > 0:
            acc = acc + v[s - 1][0]
        else:
            edge = pltpu.roll(v[T_S5 - 1][0], 1, 0)
            if is_ctx:
                prev = pltpu.roll(taps(jnp.maximum(m - 1, 0), T_S5 - 1)[0], 1, 0)
                prev = jnp.where(lax.rem(m, 2) == 1, prev, 0.0)
                edge = jnp.where(sub == 0, prev, edge)
            else:
                edge = jnp.where(lax.rem(sub, 4) == 0, 0.0, edge)
            acc = acc + edge
        if s < T_S5 - 1:
            acc = acc + v[s + 1][2]
        else:
            edge = pltpu.roll(v[0][2], J_S5 - 1, 0)
            if is_ctx:
                nxt = pltpu.roll(taps(jnp.minimum(m + 1, M_BLK - 1), 0)[2], J_S5 - 1, 0)
                nxt = jnp.where(lax.rem(m, 2) == 0, nxt, 0.0)
                edge = jnp.where(sub == J_S5 - 1, nxt, edge)
            else:
                edge = jnp.where(lax.rem(sub, 4) == 3, 0.0, edge)
            acc = acc + edge
        out.append(acc)
    return out


def _ffn_kernel(x_ref, mod_ref, g_ref, wg_ref, wv_ref, cg_ref, cv_ref, wd_ref, o_ref,
                hf_scr, act_scr, *up_scrs, is_ctx):
    jf = pl.program_id(1)
    bufs = (up_scrs[0:4], up_scrs[4:8])
    half = J_S5 // 2

    def up_rows(hf, chunk0, nchunks, upg_scr, upv_scr, shg_scr, shv_scr):
        for w_ref, up_scr, sh_scr in ((wg_ref, upg_scr, shg_scr), (wv_ref, upv_scr, shv_scr)):
            up = jnp.dot(hf, w_ref[...], preferred_element_type=F32).reshape(
                nchunks, T_S5, J_S5, FF_TILE)
            up_scr[pl.ds(chunk0, nchunks)] = up
            if is_ctx:
                continue
            for c in range(nchunks):
                for s in range(T_S5):
                    turned = pltpu.roll(up[c, s], half, 0)
                    sh_scr[chunk0 + c, s, half:J_S5, :] = turned[half:J_S5]
                    sh_scr[chunk0 + c + 1, s, 0:half, :] = turned[0:half]

    @pl.when(jf == 0)
    def _():
        for i in range(N_SUB):
            rows = slice(i * SUB, (i + 1) * SUB)
            xn = _rms(x_ref[0, rows, :]) * g_ref[...]
            hf = (xn * (1.0 + mod_ref[0, 4:5, :]) + mod_ref[0, 3:4, :]).astype(BF16)
            hf_scr[rows, :] = hf
            o_ref[0, rows, :] = jnp.zeros((SUB, D_MODEL), F32)
            up_rows(hf, i * (SUB // CHUNK), SUB // CHUNK, *bufs[0])
        if not is_ctx:
            for bset in bufs:
                for sh_scr in bset[2:]:
                    sh_scr[0, :, 0:half, :] = jnp.zeros((T_S5, half, FF_TILE), F32)
                    sh_scr[M_BLK, :, half:J_S5, :] = jnp.zeros((T_S5, half, FF_TILE), F32)

    def stage_conv(rg, upg_scr, upv_scr, shg_scr, shv_scr):
        for c in range(RG // CHUNK):
            m = rg * (RG // CHUNK) + c
            for cb in range(FF_TILE // LANE):
                lanes = slice(cb * LANE, (cb + 1) * LANE)
                gate = _conv_tiles(upg_scr, shg_scr, cg_ref, m, lanes, is_ctx=is_ctx)
                val = _conv_tiles(upv_scr, shv_scr, cv_ref, m, lanes, is_ctx=is_ctx)
                act = [jax.nn.silu(g) * v for g, v in zip(gate, val)]
                for k in range(T_S5 // 2):
                    pair = jnp.concatenate([act[2 * k], act[2 * k + 1]], axis=0).astype(BF16)
                    row0 = pl.multiple_of(m * CHUNK + 2 * k * J_S5, 2 * J_S5)
                    act_scr[pl.ds(row0, 2 * J_S5), lanes] = pair

    def stage_up(rg, *bufset):
        rows = pl.ds(pl.multiple_of(rg * RG, RG), RG)
        up_rows(hf_scr[rows, :], rg * (RG // CHUNK), RG // CHUNK, *bufset)

    def run_loop(parity, do_up, do_conv):
        def body(rg, carry):
            if do_conv:
                stage_conv(rg, *bufs[1 - parity])
            if do_up:
                stage_up(rg, *bufs[parity])
            return carry
        lax.fori_loop(0, N_RG, body, 0)

    def down(final):
        for rg in range(N_RG):
            rows = slice(rg * RG, (rg + 1) * RG)
            d = jnp.dot(act_scr[rows, :], wd_ref[...], preferred_element_type=F32)
            if final:
                o_ref[0, rows, :] = x_ref[0, rows, :] + mod_ref[0, 5:6, :] * (o_ref[0, rows, :] + d)
            else:
                o_ref[0, rows, :] += d

    steady = jnp.logical_and(jf >= 1, jf < N_FF)
    even = lax.rem(jf, 2) == 0
    pl.when(jnp.logical_and(steady, even))(lambda: run_loop(0, True, True))
    pl.when(jnp.logical_and(steady, jnp.logical_not(even)))(lambda: run_loop(1, True, True))
    pl.when(jf == N_FF)(lambda: run_loop(N_FF % 2, False, True))
    pl.when(steady)(lambda: down(False))
    pl.when(jf == N_FF)(lambda: down(True))


def _ffn(x, mods, g_ffn, w_up, w_conv, w_down, *, layer, is_ctx):
    nblk = x.shape[0]
    last = N_FF - 1
    up_tile = lambda j: jnp.minimum(j, last)
    prev_tile = lambda j: jnp.maximum(j - 1, 0)
    up_shape = (M_BLK, T_S5, J_S5, FF_TILE)
    sh_shape = (1, 1, J_S5, LANE) if is_ctx else (M_BLK + 1, T_S5, J_S5, FF_TILE)
    buf_set = [pltpu.VMEM(up_shape, F32), pltpu.VMEM(up_shape, F32),
               pltpu.VMEM(sh_shape, F32), pltpu.VMEM(sh_shape, F32)]
    return pl.pallas_call(
        functools.partial(_ffn_kernel, is_ctx=is_ctx),
        grid=(nblk, N_FF + 1),
        in_specs=[
            pl.BlockSpec((1, BLK, D_MODEL), lambda b, j: (b, 0, 0), pipeline_mode=pl.Buffered(1)),
            _mod_spec(layer, is_ctx),
            _layer_spec((1, D_MODEL), layer),
            pl.BlockSpec((None, D_MODEL, FF_TILE), lambda b, j: (layer, 0, up_tile(j))),
            pl.BlockSpec((None, D_MODEL, FF_TILE), lambda b, j: (layer, 0, N_FF + up_tile(j))),
            pl.BlockSpec((None, 9, FF_TILE), lambda b, j: (layer, 0, prev_tile(j))),
            pl.BlockSpec((None, 9, FF_TILE), lambda b, j: (layer, 0, N_FF + prev_tile(j))),
            pl.BlockSpec((None, FF_TILE, D_MODEL), lambda b, j: (layer, prev_tile(j), 0)),
        ],
        out_specs=pl.BlockSpec((1, BLK, D_MODEL), lambda b, j: (b, 0, 0)),
        out_shape=jax.ShapeDtypeStruct(x.shape, F32),
        scratch_shapes=[pltpu.VMEM((BLK, D_MODEL), BF16), pltpu.VMEM((BLK, FF_TILE), BF16)]
        + 2 * buf_set,
        compiler_params=pltpu.CompilerParams(
            dimension_semantics=("arbitrary", "arbitrary"), vmem_limit_bytes=VMEM_LIMIT),
        name="ffn_ctx" if is_ctx else "ffn",
    )(x, mods, g_ffn, w_up, w_up, w_conv, w_conv, w_down)


def _final_kernel(x_ref, g_ref, o_ref):
    o_ref[0] = _rms(x_ref[0]) * g_ref[...]


def _final_norm(x, g):
    nblk = x.shape[0]
    return pl.pallas_call(
        _final_kernel,
        grid=(nblk, N_SUB),
        in_specs=[pl.BlockSpec((1, SUB, D_MODEL), lambda b, j: (b, j, 0)),
                  pl.BlockSpec((1, D_MODEL), lambda b, j: (0, 0))],
        out_specs=pl.BlockSpec((1, SUB, D_MODEL), lambda b, j: (b, j, 0)),
        out_shape=jax.ShapeDtypeStruct(x.shape, F32),
        name="final_norm",
    )(x, g)


def _s5_operators(a_re, a_im, b_re, b_im, c_re, c_im, log_dt, d_skip):
    hp = lax.Precision.HIGHEST
    mat = (DEPTH, SSM_GROUPS, S5_MAT, S5_MAT)
    dt = jnp.exp(log_dt)[..., None, None]
    adt_re, adt_im = a_re[..., None] * dt, a_im[..., None] * dt
    k = jnp.arange(T_S5 + 1, dtype=F32)
    mag = jnp.exp(adt_re * k)
    pw_re, pw_im = mag * jnp.cos(adt_im * k), mag * jnp.sin(adt_im * k)
    num_re, num_im = pw_re[..., 1] - 1.0, pw_im[..., 1]
    den = a_re * a_re + a_im * a_im
    cf_re = (num_re * a_re + num_im * a_im) / den
    cf_im = (num_im * a_re - num_re * a_im) / den
    bb_re = cf_re[..., None] * b_re - cf_im[..., None] * b_im
    bb_im = cf_re[..., None] * b_im + cf_im[..., None] * b_re
    width = S5_MAT

    def zflat(direction, powers):
        pr, pi = pw_re[:, direction, ..., powers, None], pw_im[:, direction, ..., powers, None]
        br, bi = bb_re[:, direction, ..., None, :], bb_im[:, direction, ..., None, :]
        shape = pr.shape[:3] + (-1,)
        return (pr * br - pi * bi).reshape(shape), (pr * bi + pi * br).reshape(shape)

    def lag_kernels(direction, z):
        return (jnp.einsum("egcp,egpx->egcx", c_re[:, direction], z[0], precision=hp)
                - jnp.einsum("egcp,egpx->egcx", c_im[:, direction], z[1], precision=hp))

    zf = zflat(0, slice(T_S5, None, -1))
    zb = zflat(1, slice(0, T_S5 + 1))
    kf, kb = lag_kernels(0, zf), lag_kernels(1, zb)
    skip = d_skip.reshape(DEPTH, SSM_GROUPS, SSM_CH)[..., None] * jnp.eye(SSM_CH, dtype=F32)
    centre = kf[..., width:width + SSM_CH] + kb[..., :SSM_CH] + skip
    band = jnp.concatenate([kf[..., SSM_CH:width], centre, kb[..., SSM_CH:width]], axis=-1)
    mt = jnp.stack([band[..., (T_S5 - 1 - t) * SSM_CH:(T_S5 - 1 - t) * SSM_CH + width]
                    for t in range(T_S5)], axis=2).reshape(mat)

    pt = jnp.stack([zf[0][..., SSM_CH:], zf[1][..., SSM_CH:], zb[0][..., :width], zb[1][..., :width]],
                   axis=2).reshape(mat)

    def cq(direction, powers):
        cr, ci = c_re[:, direction, :, None], c_im[:, direction, :, None]
        pr = jnp.moveaxis(pw_re[:, direction], -1, 2)[:, :, powers, None, :]
        pi = jnp.moveaxis(pw_im[:, direction], -1, 2)[:, :, powers, None, :]
        return cr * pr - ci * pi, cr * pi + ci * pr

    qf_re, qf_im = cq(0, slice(1, T_S5 + 1))
    qb_re, qb_im = cq(1, slice(T_S5, 0, -1))
    qt = jnp.concatenate([qf_re, -qf_im, qb_re, -qb_im], axis=-1).reshape(mat)

    a16 = jnp.stack([pw_re[:, 0, ..., T_S5], pw_im[:, 0, ..., T_S5],
                     pw_re[:, 1, ..., T_S5], pw_im[:, 1, ..., T_S5]], axis=1)
    return mt.astype(BF16), pt.astype(BF16), qt.astype(BF16), a16.reshape(DEPTH, 4, N_STATE)


def _chunk_perm():
    i = jnp.arange(CHUNK)
    return 16 * (i % J_S5) + i // J_S5


def _to_blocks(x, chunks):
    b = x.shape[0]
    x = x.reshape(b, chunks, J_S5, T_S5, D_MODEL).transpose(0, 1, 3, 2, 4)
    return x.reshape(b, chunks * CHUNK, D_MODEL)


def _from_blocks(x, chunks):
    b = x.shape[0]
    x = x.reshape(b, chunks, T_S5, J_S5, D_MODEL).transpose(0, 1, 3, 2, 4)
    return x.reshape(b, chunks * CHUNK, D_MODEL)


def kernel(x, c, ctx, c_ctx, w_ada, b_ada, g_mix, w_in, ssm_a_re, ssm_a_im, ssm_b_re, ssm_b_im,
           ssm_c_re, ssm_c_im, ssm_log_dt, ssm_d, w_glu, b_glu, g_sgu, w_spatial, b_spatial,
           w_out, g_ffn, w_up, w_conv, w_down, g_final):
    xl = _to_blocks(x, SEQ // CHUNK)
    xc = _to_blocks(ctx, CTX_LEN // CHUNK).reshape(1, BLK, D_MODEL)

    cond = jnp.concatenate([c_ctx[None], c, jnp.zeros((7, D_MODEL), F32)], axis=0)
    mods = _modulation(cond, w_ada, b_ada).reshape(DEPTH, 16, 6, D_MODEL)

    mt, pt, qt, avec = _s5_operators(ssm_a_re, ssm_a_im, ssm_b_re, ssm_b_im, ssm_c_re, ssm_c_im,
                                     ssm_log_dt, ssm_d)
    perm = _chunk_perm()
    ws = w_spatial[:, :, perm][:, :, :, perm].astype(BF16)
    bs = b_spatial[:, :, perm][..., None]
    w_in_b, w_glu_b, w_out_b = w_in.astype(BF16), w_glu.astype(BF16), w_out.astype(BF16)
    w_up_b, w_down_b = w_up.astype(BF16), w_down.astype(BF16)
    w_conv9 = w_conv.reshape(DEPTH, 9, 2 * D_FF)
    gm, gs, gf = g_mix[:, None], g_sgu[:, None], g_ffn[:, None]
    bg = b_glu[:, None]
    zero_h0 = jnp.zeros((1, 4, N_STATE), F32)

    for i in range(DEPTH):
        need_ctx = i < DEPTH - 1
        r_c, sgu_c = _mixer_in(xc, mods, gm, w_in_b, gs, ws, bs, layer=i, is_ctx=True,
                               with_sgu=need_ctx)
        y_c, hfin = _s5_core(r_c, mt, pt, qt, avec, zero_h0, layer=i, is_ctx=True, with_y=need_ctx)
        h0 = hfin.transpose(1, 0, 2)

        r_l, sgu_l = _mixer_in(xl, mods, gm, w_in_b, gs, ws, bs, layer=i, is_ctx=False,
                               with_sgu=True)
        y_l, _ = _s5_core(r_l, mt, pt, qt, avec, h0, layer=i, is_ctx=False, with_y=True)
        xl = _mixer_out(xl, y_l, sgu_l, mods, w_glu_b, bg, w_out_b, layer=i, is_ctx=False)
        xl = _ffn(xl, mods, gf, w_up_b, w_conv9, w_down_b, layer=i, is_ctx=False)

        if need_ctx:
            xc = _mixer_out(xc, y_c, sgu_c, mods, w_glu_b, bg, w_out_b, layer=i, is_ctx=True)
            xc = _ffn(xc, mods, gf, w_up_b, w_conv9, w_down_b, layer=i, is_ctx=True)

    out = _final_norm(xl, g_final[None])
    return _from_blocks(out, SEQ // CHUNK)
```
